```python
import jax, jax.numpy as jnp
from jax import lax
import numpy as np


D_MODEL = 1024
BATCH = 8
SEQ = 4096
DEPTH = 1
DEC_BATCH = 4
DEC_SEQ = 8192
PAST_LEN = 128

HEAD_DIM = 128
N_Q_HEADS = D_MODEL // HEAD_DIM
N_KV_HEADS = 2
Q_PER_KV = N_Q_HEADS // N_KV_HEADS
WINDOW = 128
BLOCK = 128
ROPE_THETA = 10000.0
SG_WIDTH = D_MODEL
SG_GROUPS = 4
SG_GROUP_DIM = SG_WIDTH // SG_GROUPS
CHUNK = 128
D_FF = -(-8 * D_MODEL // (3 * 256)) * 256
Q_W = N_Q_HEADS * HEAD_DIM
KV_W = N_KV_HEADS * HEAD_DIM
D_IN = 2 * SG_WIDTH + Q_W + 2 * KV_W + 2 * D_MODEL
RMS_EPS = 1e-6
LN_EPS = 1e-5
NEG = -1e30

kernel_name = "gated_gmlp_swa_hybrid_encoder"


def rmsnorm(x, g):
    xf = x.astype(jnp.float32)
    y = xf * lax.rsqrt(jnp.mean(xf * xf, axis=-1, keepdims=True) + RMS_EPS)
    return (y * g.astype(jnp.float32)).astype(x.dtype)


def layernorm(x, g, b):
    xf = x.astype(jnp.float32)
    mu = jnp.mean(xf, axis=-1, keepdims=True)
    xc = xf - mu
    y = xc * lax.rsqrt(jnp.mean(xc * xc, axis=-1, keepdims=True) + LN_EPS)
    return (y * g.astype(jnp.float32) + b.astype(jnp.float32)).astype(x.dtype)


def rope(x):
    s = x.shape[1]
    half = HEAD_DIM // 2
    inv = ROPE_THETA ** (-jnp.arange(half, dtype=jnp.float32) / half)
    ang = jnp.arange(s, dtype=jnp.float32)[:, None] * inv[None, :]
    cos = jnp.cos(ang)[None, :, None, :]
    sin = jnp.sin(ang)[None, :, None, :]
    xf = x.astype(jnp.float32)
    x1, x2 = xf[..., :half], xf[..., half:]
    return jnp.concatenate([x1 * cos - x2 * sin, x2 * cos + x1 * sin], axis=-1).astype(x.dtype)


def spatial_gating(u, v, ln_g, ln_b, w_s, b_s):
    bsz, s, _ = u.shape
    n = s // CHUNK
    v = layernorm(v, ln_g, ln_b)
    vr = v.reshape(bsz, n, CHUNK, SG_GROUPS, SG_GROUP_DIM)
    mixed = jnp.einsum('gpq,bnqgc->bnpgc', w_s, vr) + b_s.T[:, :, None]
    return (u.reshape(bsz, n, CHUNK, SG_GROUPS, SG_GROUP_DIM) * mixed).reshape(bsz, s, SG_WIDTH)


def band(t, n):
    tp = jnp.pad(t, ((0, 0), (BLOCK, BLOCK), (0, 0), (0, 0)))
    tp = tp.reshape(t.shape[0], n + 2, BLOCK, t.shape[2], t.shape[3])
    return jnp.concatenate([tp[:, :-2], tp[:, 1:-1], tp[:, 2:]], axis=2)


def windowed_attention(q, k, v, sink):
    bsz, s = q.shape[0], q.shape[1]
    n = s // BLOCK
    q = rope(q)
    k = rope(k)
    qb = q.reshape(bsz, n, BLOCK, N_KV_HEADS, Q_PER_KV, HEAD_DIM)
    kb = band(k, n)
    vb = band(v, n)
    sc = jnp.einsum('bnqhgd,bnkhd->bnhgqk', qb, kb,
                    preferred_element_type=jnp.float32) * (HEAD_DIM ** -0.5)
    qi = jnp.arange(BLOCK)
    kj = jnp.arange(3 * BLOCK)
    rel = kj[None, :] - BLOCK - qi[:, None]
    kpos = jnp.arange(n)[:, None] * BLOCK + kj[None, :] - BLOCK
    mask = (jnp.abs(rel) <= WINDOW)[None] & ((kpos >= 0) & (kpos < s))[:, None, :]
    sc = jnp.where(mask[None, :, None, None], sc, NEG)
    sl = sink.astype(jnp.float32).reshape(N_KV_HEADS, Q_PER_KV)[None, None, :, :, None, None]
    m = jnp.maximum(jnp.max(sc, axis=-1, keepdims=True), sl)
    p = jnp.exp(sc - m)
    denom = jnp.sum(p, axis=-1, keepdims=True) + jnp.exp(sl - m)
    o = jnp.einsum('bnhgqk,bnkhd->bnqhgd', (p / denom).astype(v.dtype), vb)
    return o.reshape(bsz, s, Q_W)


def encoder_layer(x, g_mix, w_in, sg_ln_g, sg_ln_b, sg_w, sg_b, attn_sink,
                  w_a, w_b, w_out, g_ffn, w_gate, w_up, w_down):
    bsz, s, _ = x.shape
    h = rmsnorm(x, g_mix)
    z = h @ w_in
    cuts = np.cumsum([SG_WIDTH, SG_WIDTH, Q_W, KV_W, KV_W, D_MODEL]).tolist()
    u, v, q, k, va, ga, gb = jnp.split(z, cuts, axis=-1)
    a = spatial_gating(u, v, sg_ln_g, sg_ln_b, sg_w, sg_b) @ w_a
    b = windowed_attention(q.reshape(bsz, s, N_Q_HEADS, HEAD_DIM),
                           k.reshape(bsz, s, N_KV_HEADS, HEAD_DIM),
                           va.reshape(bsz, s, N_KV_HEADS, HEAD_DIM),
                           attn_sink) @ w_b
    merged = jax.nn.sigmoid(ga) * a + jax.nn.sigmoid(gb) * b
    x = x + merged @ w_out
    h = rmsnorm(x, g_ffn)
    x = x + (jax.nn.silu(h @ w_gate) * (h @ w_up)) @ w_down
    return x


def trunk(x, g_mix, w_in, sg_ln_g, sg_ln_b, sg_w, sg_b, attn_sink,
          w_a, w_b, w_out, g_ffn, w_gate, w_up, w_down, g_final):
    for l in range(DEPTH):
        x = encoder_layer(x, g_mix[l], w_in[l], sg_ln_g[l], sg_ln_b[l], sg_w[l], sg_b[l],
                          attn_sink[l], w_a[l], w_b[l], w_out[l], g_ffn[l],
                          w_gate[l], w_up[l], w_down[l])
    return rmsnorm(x, g_final)


def setup_inputs(seed: int = 0) -> dict:
    key = jax.random.key(seed)
    ks = jax.random.split(key, 20)
    f32 = jnp.float32
    nrm = lambda k, shp, sc: jax.random.normal(k, shp, f32) * sc
    L = DEPTH
    return {
        "x_prompt": nrm(ks[0], (BATCH, SEQ, D_MODEL), 1.0),
        "x_sample": nrm(ks[1], (DEC_BATCH, DEC_SEQ, D_MODEL), 1.0),
        "g_mix": 1.0 + nrm(ks[2], (L, D_MODEL), 0.02),
        "w_in": nrm(ks[3], (L, D_MODEL, D_IN), D_MODEL ** -0.5),
        "sg_ln_g": 1.0 + nrm(ks[4], (L, SG_WIDTH), 0.02),
        "sg_ln_b": nrm(ks[5], (L, SG_WIDTH), 0.02),
        "sg_w": nrm(ks[6], (L, SG_GROUPS, CHUNK, CHUNK), CHUNK ** -0.5),
        "sg_b": 1.0 + nrm(ks[7], (L, SG_GROUPS, CHUNK), 0.02),
        "attn_sink": nrm(ks[8], (L, N_Q_HEADS), 0.5),
        "w_a": nrm(ks[9], (L, SG_WIDTH, D_MODEL), SG_WIDTH ** -0.5),
        "w_b": nrm(ks[10], (L, Q_W, D_MODEL), Q_W ** -0.5),
        "w_out": nrm(ks[11], (L, D_MODEL, D_MODEL), D_MODEL ** -0.5),
        "g_ffn": 1.0 + nrm(ks[12], (L, D_MODEL), 0.02),
        "w_gate": nrm(ks[13], (L, D_MODEL, D_FF), D_MODEL ** -0.5),
        "w_up": nrm(ks[14], (L, D_MODEL, D_FF), D_MODEL ** -0.5),
        "w_down": nrm(ks[15], (L, D_FF, D_MODEL), D_FF ** -0.5),
        "g_final": 1.0 + nrm(ks[16], (D_MODEL,), 0.02),
    }


def reference(x_prompt, x_sample, g_mix, w_in, sg_ln_g, sg_ln_b, sg_w, sg_b, attn_sink,
              w_a, w_b, w_out, g_ffn, w_gate, w_up, w_down, g_final):
    y_prompt = trunk(x_prompt, g_mix, w_in, sg_ln_g, sg_ln_b, sg_w, sg_b, attn_sink,
                     w_a, w_b, w_out, g_ffn, w_gate, w_up, w_down, g_final)
    y_sample = trunk(x_sample, g_mix, w_in, sg_ln_g, sg_ln_b, sg_w, sg_b, attn_sink,
                     w_a, w_b, w_out, g_ffn, w_gate, w_up, w_down, g_final)
    return (y_prompt, y_sample)
```

```python
import functools

import jax
import jax.numpy as jnp
import numpy as np
from jax import lax
from jax.experimental import pallas as pl
from jax.experimental.pallas import tpu as pltpu

D_MODEL = 1024
HEAD_DIM = 128
N_Q_HEADS = D_MODEL // HEAD_DIM
N_KV_HEADS = 2
Q_PER_KV = N_Q_HEADS // N_KV_HEADS
WINDOW = 128
BLOCK = 128
ROPE_THETA = 10000.0
SG_WIDTH = D_MODEL
SG_GROUPS = 4
SG_GROUP_DIM = SG_WIDTH // SG_GROUPS
CHUNK = 128
D_FF = -(-8 * D_MODEL // (3 * 256)) * 256
Q_W = N_Q_HEADS * HEAD_DIM
KV_W = N_KV_HEADS * HEAD_DIM
D_IN = 2 * SG_WIDTH + Q_W + 2 * KV_W + 2 * D_MODEL
RMS_EPS = 1e-6
LN_EPS = 1e-5
NEG = -1e30

OFF_U = 0
OFF_V = OFF_U + SG_WIDTH
OFF_Q = OFF_V + SG_WIDTH
OFF_K = OFF_Q + Q_W
OFF_VA = OFF_K + KV_W
OFF_GA = OFF_VA + KV_W
OFF_GB = OFF_GA + D_MODEL

V7X_VMEM_BYTES = 64 * 1024 * 1024
VMEM_LIMIT_BYTES = V7X_VMEM_BYTES - 8 * 1024 * 1024

TOKEN_TILE = 512
FF_CHUNK = D_FF // 2

BF16 = jnp.bfloat16
F32 = jnp.float32


def _const_spec(shape):
    nd = len(shape)
    return pl.BlockSpec(shape, lambda *_: (0,) * nd, pipeline_mode=pl.Buffered(1))


def _rms_scale(x):
    return lax.rsqrt(jnp.mean(x * x, axis=-1, keepdims=True) + RMS_EPS)


def _inproj_kernel(x_ref, gmix_ref, w_ref, lng_ref, lnb_ref, cos_ref, sin_ref,
                   u_ref, v_ref, q_ref, k_ref, va_ref, ga_ref, gb_ref):
    x = x_ref[...]
    h = (x * _rms_scale(x) * gmix_ref[...]).astype(BF16)

    def proj(off, width):
        return jnp.dot(h, w_ref[:, off:off + width], preferred_element_type=F32)

    u_ref[...] = proj(OFF_U, SG_WIDTH).astype(BF16)

    v = proj(OFF_V, SG_WIDTH)
    mu = jnp.mean(v, axis=-1, keepdims=True)
    vc = v - mu
    vn = vc * lax.rsqrt(jnp.mean(vc * vc, axis=-1, keepdims=True) + LN_EPS)
    v_ref[...] = (vn * lng_ref[...] + lnb_ref[...]).astype(BF16)

    cos = cos_ref[...]
    sin = sin_ref[...]

    def rope(t, c, s):
        return t * c + pltpu.roll(t, HEAD_DIM // 2, axis=1) * s

    scale = HEAD_DIM ** -0.5
    cos_q = cos * scale
    sin_q = sin * scale
    q = proj(OFF_Q, Q_W)
    for hd in range(N_Q_HEADS):
        sl = slice(hd * HEAD_DIM, (hd + 1) * HEAD_DIM)
        q_ref[:, sl] = rope(q[:, sl], cos_q, sin_q).astype(BF16)

    kv = proj(OFF_K, 2 * KV_W)
    for hd in range(N_KV_HEADS):
        sl = slice(hd * HEAD_DIM, (hd + 1) * HEAD_DIM)
        k_ref[:, sl] = rope(kv[:, sl], cos, sin).astype(BF16)
    va_ref[...] = kv[:, KV_W:].astype(BF16)

    ga_ref[...] = jax.nn.sigmoid(proj(OFF_GA, D_MODEL)).astype(BF16)
    gb_ref[...] = jax.nn.sigmoid(proj(OFF_GB, D_MODEL)).astype(BF16)


def _inproj(x2d, seq, g_mix, w_in, ln_g, ln_b, cos, sin):
    n = x2d.shape[0]
    t = TOKEN_TILE
    tiles_per_seq = seq // t
    tok = lambda w: pl.BlockSpec((t, w), lambda i: (i, 0))
    pos = pl.BlockSpec((t, HEAD_DIM), lambda i: (i % tiles_per_seq, 0))
    out_widths = (SG_WIDTH, SG_WIDTH, Q_W, KV_W, KV_W, D_MODEL, D_MODEL)
    return pl.pallas_call(
        _inproj_kernel,
        grid=(n // t,),
        in_specs=[tok(D_MODEL), _const_spec((1, D_MODEL)), _const_spec((D_MODEL, D_IN)),
                  _const_spec((1, SG_WIDTH)), _const_spec((1, SG_WIDTH)), pos, pos],
        out_specs=[tok(w) for w in out_widths],
        out_shape=[jax.ShapeDtypeStruct((n, w), BF16) for w in out_widths],
        compiler_params=pltpu.CompilerParams(
            dimension_semantics=("parallel",), vmem_limit_bytes=VMEM_LIMIT_BYTES),
        name="inproj",
    )(x2d, g_mix, w_in, ln_g, ln_b, cos, sin)


def _mixer_kernel(sink_ref, x_ref, u_ref, v_ref, q_ref,
                  kc_ref, kp_ref, kn_ref, vc_ref, vp_ref, vn_ref, ga_ref, gb_ref,
                  ws_ref, bs_ref, wa_ref, wb_ref, wout_ref, o_ref,
                  kfull, vfull, attn_scr, sg_scr):
    t = TOKEN_TILE
    nblk = t // BLOCK
    i = pl.program_id(1)
    first_tile = i == 0
    last_tile = i == pl.num_programs(1) - 1

    for c in range(t // CHUNK):
        rows = slice(c * CHUNK, (c + 1) * CHUNK)
        for g in range(SG_GROUPS):
            cols = slice(g * SG_GROUP_DIM, (g + 1) * SG_GROUP_DIM)
            mixed = jnp.dot(ws_ref[g], v_ref[rows, cols], preferred_element_type=F32)
            mixed = mixed + bs_ref[:, g:g + 1]
            sg_scr[rows, cols] = (u_ref[rows, cols].astype(F32) * mixed).astype(BF16)

    kfull[0:BLOCK, :] = kp_ref[...]
    kfull[BLOCK:BLOCK + t, :] = kc_ref[...]
    kfull[BLOCK + t:, :] = kn_ref[...]
    vfull[0:BLOCK, :] = vp_ref[...]
    vfull[BLOCK:BLOCK + t, :] = vc_ref[...]
    vfull[BLOCK + t:, :] = vn_ref[...]

    rows_q = Q_PER_KV * BLOCK
    qi = lax.broadcasted_iota(jnp.int32, (rows_q, BLOCK), 0) % BLOCK
    kj = lax.broadcasted_iota(jnp.int32, (rows_q, BLOCK), 1)
    prev_ok = kj >= qi
    next_ok = kj <= qi

    prev_ok_first = kj >= qi + jnp.where(first_tile, BLOCK, 0)
    next_ok_last = kj <= qi - jnp.where(last_tile, BLOCK, 0)

    for j in range(nblk):
        qrows = slice(j * BLOCK, (j + 1) * BLOCK)
        p_ok = prev_ok_first if j == 0 else prev_ok
        n_ok = next_ok_last if j == nblk - 1 else next_ok
        for hk in range(N_KV_HEADS):
            kcols = slice(hk * HEAD_DIM, (hk + 1) * HEAD_DIM)
            heads = [hk * Q_PER_KV + g for g in range(Q_PER_KV)]
            qs = jnp.concatenate(
                [q_ref[qrows, hd * HEAD_DIM:(hd + 1) * HEAD_DIM] for hd in heads], axis=0)
            kk = kfull[j * BLOCK:(j + 3) * BLOCK, kcols]
            vv = vfull[j * BLOCK:(j + 3) * BLOCK, kcols]
            s = lax.dot_general(qs, kk, (((1,), (1,)), ((), ())),
                                preferred_element_type=F32)
            s_p = jnp.where(p_ok, s[:, :BLOCK], NEG)
            s_c = s[:, BLOCK:2 * BLOCK]
            s_n = jnp.where(n_ok, s[:, 2 * BLOCK:], NEG)
            sink = jnp.concatenate(
                [jnp.full((BLOCK, 1), sink_ref[hd], F32) for hd in heads], axis=0)
            m = jnp.maximum(jnp.maximum(s_p, s_c), s_n).max(axis=-1, keepdims=True)
            m = jnp.maximum(m, sink)
            p_p = jnp.exp(s_p - m)
            p_c = jnp.exp(s_c - m)
            p_n = jnp.exp(s_n - m)
            denom = (p_p + p_c + p_n).sum(axis=-1, keepdims=True) + jnp.exp(sink - m)
            p = jnp.concatenate([p_p, p_c, p_n], axis=1).astype(BF16)
            o = jnp.dot(p, vv, preferred_element_type=F32) / denom
            for g, hd in enumerate(heads):
                attn_scr[qrows, hd * HEAD_DIM:(hd + 1) * HEAD_DIM] = (
                    o[g * BLOCK:(g + 1) * BLOCK].astype(BF16))

    a = jnp.dot(sg_scr[...], wa_ref[...], preferred_element_type=F32)
    b = jnp.dot(attn_scr[...], wb_ref[...], preferred_element_type=F32)
    merged = (ga_ref[...].astype(F32) * a + gb_ref[...].astype(F32) * b).astype(BF16)
    o_ref[...] = x_ref[...] + jnp.dot(merged, wout_ref[...], preferred_element_type=F32)


def _mixer(x2d, seq, u, v, q, k, va, ga, gb, sg_w, sg_bt, sink, w_a, w_b, w_out):
    n = x2d.shape[0]
    t = TOKEN_TILE
    tiles_per_seq = seq // t
    blocks_per_tile = t // BLOCK
    blocks_per_seq = seq // BLOCK
    bsz = n // seq

    def tok(w):
        return pl.BlockSpec((t, w), lambda b, i: (b * tiles_per_seq + i, 0))

    def halo_prev(w):
        return pl.BlockSpec(
            (BLOCK, w),
            lambda b, i: (b * blocks_per_seq + jnp.maximum(i * blocks_per_tile - 1, 0), 0))

    def halo_next(w):
        return pl.BlockSpec(
            (BLOCK, w),
            lambda b, i: (b * blocks_per_seq
                          + jnp.minimum((i + 1) * blocks_per_tile, blocks_per_seq - 1), 0))

    return pl.pallas_call(
        _mixer_kernel,
        grid=(bsz, tiles_per_seq),
        in_specs=[pl.BlockSpec(memory_space=pltpu.SMEM),
                  tok(D_MODEL), tok(SG_WIDTH), tok(SG_WIDTH), tok(Q_W),
                  tok(KV_W), halo_prev(KV_W), halo_next(KV_W),
                  tok(KV_W), halo_prev(KV_W), halo_next(KV_W),
                  tok(D_MODEL), tok(D_MODEL),
                  _const_spec((SG_GROUPS, CHUNK, CHUNK)), _const_spec((CHUNK, SG_GROUPS)),
                  _const_spec((SG_WIDTH, D_MODEL)), _const_spec((Q_W, D_MODEL)),
                  _const_spec((D_MODEL, D_MODEL))],
        out_specs=tok(D_MODEL),
        scratch_shapes=[pltpu.VMEM((t + 2 * BLOCK, KV_W), BF16),
                        pltpu.VMEM((t + 2 * BLOCK, KV_W), BF16),
                        pltpu.VMEM((t, Q_W), BF16),
                        pltpu.VMEM((t, SG_WIDTH), BF16)],
        out_shape=jax.ShapeDtypeStruct((n, D_MODEL), F32),
        compiler_params=pltpu.CompilerParams(
            dimension_semantics=("parallel", "parallel"), vmem_limit_bytes=VMEM_LIMIT_BYTES),
        name="mixer",
    )(sink, x2d, u, v, q, k, k, k, va, va, va, ga, gb, sg_w, sg_bt, w_a, w_b, w_out)


def _ffn_kernel(x_ref, gffn_ref, wg_ref, wu_ref, wd_ref, gfin_ref, o_ref):
    x = x_ref[...]
    h = (x * _rms_scale(x) * gffn_ref[...]).astype(BF16)
    acc = x
    for c in range(D_FF // FF_CHUNK):
        cols = slice(c * FF_CHUNK, (c + 1) * FF_CHUNK)
        gate = jnp.dot(h, wg_ref[:, cols], preferred_element_type=F32)
        up = jnp.dot(h, wu_ref[:, cols], preferred_element_type=F32)
        act = (jax.nn.silu(gate) * up).astype(BF16)
        acc = acc + jnp.dot(act, wd_ref[cols, :], preferred_element_type=F32)
    o_ref[...] = acc * _rms_scale(acc) * gfin_ref[...]


def _ffn(x2d, g_ffn, w_gate, w_up, w_down, g_final):
    n = x2d.shape[0]
    t = TOKEN_TILE
    tok = pl.BlockSpec((t, D_MODEL), lambda i: (i, 0))
    return pl.pallas_call(
        _ffn_kernel,
        grid=(n // t,),
        in_specs=[tok, _const_spec((1, D_MODEL)), _const_spec((D_MODEL, D_FF)),
                  _const_spec((D_MODEL, D_FF)), _const_spec((D_FF, D_MODEL)),
                  _const_spec((1, D_MODEL))],
        out_specs=tok,
        out_shape=jax.ShapeDtypeStruct((n, D_MODEL), F32),
        compiler_params=pltpu.CompilerParams(
            dimension_semantics=("parallel",), vmem_limit_bytes=VMEM_LIMIT_BYTES),
        name="ffn",
    )(x2d, g_ffn, w_gate, w_up, w_down, g_final)


def _rope_tables(seq):
    half = HEAD_DIM // 2
    inv = ROPE_THETA ** (-jnp.arange(half, dtype=F32) / half)
    ang = jnp.arange(seq, dtype=F32)[:, None] * inv[None, :]
    cos = jnp.cos(ang)
    sin = jnp.sin(ang)
    return jnp.concatenate([cos, cos], axis=-1), jnp.concatenate([-sin, sin], axis=-1)


def _trunk(x, p):
    bsz, seq, _ = x.shape
    assert seq % TOKEN_TILE == 0 and TOKEN_TILE % BLOCK == 0 and TOKEN_TILE % CHUNK == 0
    x2d = x.reshape(bsz * seq, D_MODEL)
    cos, sin = _rope_tables(seq)
    u, v, q, k, va, ga, gb = _inproj(x2d, seq, p["g_mix"], p["w_in"], p["ln_g"], p["ln_b"],
                                     cos, sin)
    x2d = _mixer(x2d, seq, u, v, q, k, va, ga, gb, p["sg_w"], p["sg_bt"], p["sink"],
                 p["w_a"], p["w_b"], p["w_out"])
    x2d = _ffn(x2d, p["g_ffn"], p["w_gate"], p["w_up"], p["w_down"], p["g_final"])
    return x2d.reshape(bsz, seq, D_MODEL)


def kernel(x_prompt, x_sample, g_mix, w_in, sg_ln_g, sg_ln_b, sg_w, sg_b, attn_sink,
           w_a, w_b, w_out, g_ffn, w_gate, w_up, w_down, g_final):
    assert w_in.shape[0] == 1, "single layer: the final rmsnorm is fused into its channel mixer"
    row = lambda a: a.reshape(1, -1).astype(F32)
    p = dict(
        g_mix=row(g_mix[0]), w_in=w_in[0].astype(BF16),
        ln_g=row(sg_ln_g[0]), ln_b=row(sg_ln_b[0]),
        sg_w=sg_w[0].astype(BF16), sg_bt=sg_b[0].T.astype(F32),
        sink=attn_sink[0].astype(F32),
        w_a=w_a[0].astype(BF16), w_b=w_b[0].astype(BF16), w_out=w_out[0].astype(BF16),
        g_ffn=row(g_ffn[0]), w_gate=w_gate[0].astype(BF16), w_up=w_up[0].astype(BF16),
        w_down=w_down[0].astype(BF16), g_final=row(g_final))
    return (_trunk(x_prompt, p), _trunk(x_sample, p))
```

```python
import functools

import jax
import jax.numpy as jnp
import numpy as np
from jax import lax
from jax.experimental import pallas as pl
from jax.experimental.pallas import tpu as pltpu

D_MODEL = 1024
HEAD_DIM = 128
N_Q_HEADS = D_MODEL // HEAD_DIM
N_KV_HEADS = 2
Q_PER_KV = N_Q_HEADS // N_KV_HEADS
WINDOW = 128
BLOCK = 128
ROPE_THETA = 10000.0
SG_WIDTH = D_MODEL
SG_GROUPS = 4
SG_GROUP_DIM = SG_WIDTH // SG_GROUPS
CHUNK = 128
D_FF = -(-8 * D_MODEL // (3 * 256)) * 256
Q_W = N_Q_HEADS * HEAD_DIM
KV_W = N_KV_HEADS * HEAD_DIM
D_IN = 2 * SG_WIDTH + Q_W + 2 * KV_W + 2 * D_MODEL
RMS_EPS = 1e-6
LN_EPS = 1e-5
NEG = -1e30
LOG2_E = 1.4426950408889634

OFF_U = 0
OFF_V = OFF_U + SG_WIDTH
OFF_Q = OFF_V + SG_WIDTH
OFF_K = OFF_Q + Q_W
OFF_VA = OFF_K + KV_W
OFF_GA = OFF_VA + KV_W
OFF_GB = OFF_GA + D_MODEL

V7X_VMEM_BYTES = 64 * 1024 * 1024
VMEM_LIMIT_BYTES = V7X_VMEM_BYTES - 8 * 1024 * 1024

INPROJ_TILE = 1024
MIXER_TILE = 1024
FFN_TILE = 1024
V7X_MXU_DIM = 256
FF_TILES = D_FF // V7X_MXU_DIM
FF_BOUNDS = (0, (FF_TILES // 2) * V7X_MXU_DIM, D_FF)

BF16 = jnp.bfloat16
F32 = jnp.float32


def _const_spec(shape):
    nd = len(shape)
    return pl.BlockSpec(shape, lambda *_: (0,) * nd, pipeline_mode=pl.Buffered(1))


def _rms_scale(x):
    return lax.rsqrt(jnp.mean(x * x, axis=-1, keepdims=True) + RMS_EPS)


def _inproj_kernel(x_ref, gmix_ref, w_ref, lng_ref, lnb_ref, cos_ref, sin_ref,
                   u_ref, v_ref, q_ref, k_ref, va_ref, ga_ref, gb_ref):
    x = x_ref[...]
    h = (x * _rms_scale(x) * gmix_ref[...]).astype(BF16)

    def proj(off, width):
        return jnp.dot(h, w_ref[:, off:off + width], preferred_element_type=F32)

    u_ref[...] = proj(OFF_U, SG_WIDTH).astype(BF16)

    v = proj(OFF_V, SG_WIDTH)
    mu = jnp.mean(v, axis=-1, keepdims=True)
    vc = v - mu
    vn = vc * lax.rsqrt(jnp.mean(vc * vc, axis=-1, keepdims=True) + LN_EPS)
    v_ref[...] = (vn * lng_ref[...] + lnb_ref[...]).astype(BF16)

    cos = cos_ref[...]
    sin = sin_ref[...]

    def rope(t, c, s):
        return t * c + pltpu.roll(t, HEAD_DIM // 2, axis=1) * s

    scale = HEAD_DIM ** -0.5 * LOG2_E
    cos_q = cos * scale
    sin_q = sin * scale
    q = proj(OFF_Q, Q_W)
    for hd in range(N_Q_HEADS):
        sl = slice(hd * HEAD_DIM, (hd + 1) * HEAD_DIM)
        q_ref[:, sl] = rope(q[:, sl], cos_q, sin_q).astype(BF16)

    kv = proj(OFF_K, 2 * KV_W)
    for hd in range(N_KV_HEADS):
        sl = slice(hd * HEAD_DIM, (hd + 1) * HEAD_DIM)
        k_ref[:, sl] = rope(kv[:, sl], cos, sin).astype(BF16)
    va_ref[...] = kv[:, KV_W:].astype(BF16)

    ga_ref[...] = jax.nn.sigmoid(proj(OFF_GA, D_MODEL)).astype(BF16)
    gb_ref[...] = jax.nn.sigmoid(proj(OFF_GB, D_MODEL)).astype(BF16)


def _inproj(x2d, seq, g_mix, w_in, ln_g, ln_b, cos, sin):
    n = x2d.shape[0]
    t = INPROJ_TILE
    tiles_per_seq = seq // t
    tok = lambda w: pl.BlockSpec((t, w), lambda i: (i, 0))
    pos = pl.BlockSpec((t, HEAD_DIM), lambda i: (i % tiles_per_seq, 0))
    out_widths = (SG_WIDTH, SG_WIDTH, Q_W, KV_W, KV_W, D_MODEL, D_MODEL)
    return pl.pallas_call(
        _inproj_kernel,
        grid=(n // t,),
        in_specs=[tok(D_MODEL), _const_spec((1, D_MODEL)), _const_spec((D_MODEL, D_IN)),
                  _const_spec((1, SG_WIDTH)), _const_spec((1, SG_WIDTH)), pos, pos],
        out_specs=[tok(w) for w in out_widths],
        out_shape=[jax.ShapeDtypeStruct((n, w), BF16) for w in out_widths],
        compiler_params=pltpu.CompilerParams(
            dimension_semantics=("parallel",), vmem_limit_bytes=VMEM_LIMIT_BYTES),
        name="inproj",
    )(x2d, g_mix, w_in, ln_g, ln_b, cos, sin)


def _mixer_kernel(sink_ref, x_ref, u_ref, v_ref, q_ref,
                  kc_ref, kp_ref, kn_ref, vc_ref, vp_ref, vn_ref, ga_ref, gb_ref,
                  ws_ref, bs_ref, wa_ref, wb_ref, wout_ref, o_ref,
                  kfull, vfull, attn_scr, sg_scr):
    t = MIXER_TILE
    nblk = t // BLOCK
    i = pl.program_id(1)
    first_tile = i == 0
    last_tile = i == pl.num_programs(1) - 1

    assert BLOCK == CHUNK

    def blk(j):
        return slice(j * BLOCK, (j + 1) * BLOCK)

    def spatial_gate(j):
        rows = blk(j)
        for g in range(SG_GROUPS):
            cols = slice(g * SG_GROUP_DIM, (g + 1) * SG_GROUP_DIM)
            mixed = jnp.dot(ws_ref[g], v_ref[rows, cols], preferred_element_type=F32)
            mixed = mixed + bs_ref[:, g:g + 1]
            sg_scr[rows, cols] = (u_ref[rows, cols].astype(F32) * mixed).astype(BF16)

    kfull[0:BLOCK, :] = kp_ref[...]
    kfull[BLOCK:BLOCK + t, :] = kc_ref[...]
    kfull[BLOCK + t:, :] = kn_ref[...]
    vfull[0:BLOCK, :] = vp_ref[...]
    vfull[BLOCK:BLOCK + t, :] = vc_ref[...]
    vfull[BLOCK + t:, :] = vn_ref[...]

    rows_q = Q_PER_KV * BLOCK
    qi = lax.broadcasted_iota(jnp.int32, (rows_q, BLOCK), 0) % BLOCK
    kj = lax.broadcasted_iota(jnp.int32, (rows_q, BLOCK), 1)
    prev_ok = kj >= qi
    next_ok = kj <= qi

    prev_ok_first = kj >= qi + jnp.where(first_tile, BLOCK, 0)
    next_ok_last = kj <= qi - jnp.where(last_tile, BLOCK, 0)

    def heads_of(hk):
        return [hk * Q_PER_KV + g for g in range(Q_PER_KV)]

    def scores(j, hk):
        qs = jnp.concatenate(
            [q_ref[blk(j), hd * HEAD_DIM:(hd + 1) * HEAD_DIM] for hd in heads_of(hk)], axis=0)
        kk = kfull[j * BLOCK:(j + 3) * BLOCK, hk * HEAD_DIM:(hk + 1) * HEAD_DIM]
        return lax.dot_general(qs, kk, (((1,), (1,)), ((), ())), preferred_element_type=F32)

    def attend(j, hk, s):
        p_ok = prev_ok_first if j == 0 else prev_ok
        n_ok = next_ok_last if j == nblk - 1 else next_ok
        heads = heads_of(hk)
        s_p = jnp.where(p_ok, s[:, :BLOCK], NEG)
        s_c = s[:, BLOCK:2 * BLOCK]
        s_n = jnp.where(n_ok, s[:, 2 * BLOCK:], NEG)
        sink = jnp.concatenate(
            [jnp.full((BLOCK, 1), sink_ref[hd] * LOG2_E, F32) for hd in heads], axis=0)
        m = jnp.maximum(jnp.maximum(s_p, s_c), s_n).max(axis=-1, keepdims=True)
        m = jnp.maximum(m, sink)
        p_p = jnp.exp2(s_p - m)
        p_c = jnp.exp2(s_c - m)
        p_n = jnp.exp2(s_n - m)
        denom = (p_p + p_c + p_n).sum(axis=-1, keepdims=True) + jnp.exp2(sink - m)
        p = jnp.concatenate([p_p, p_c, p_n], axis=1).astype(BF16)
        vv = vfull[j * BLOCK:(j + 3) * BLOCK, hk * HEAD_DIM:(hk + 1) * HEAD_DIM]
        o = jnp.dot(p, vv, preferred_element_type=F32) / denom
        for g, hd in enumerate(heads):
            attn_scr[blk(j), hd * HEAD_DIM:(hd + 1) * HEAD_DIM] = (
                o[g * BLOCK:(g + 1) * BLOCK].astype(BF16))

    def proj_a(j):
        return jnp.dot(sg_scr[blk(j), :], wa_ref[...], preferred_element_type=F32)

    def proj_b(j):
        return jnp.dot(attn_scr[blk(j), :], wb_ref[...], preferred_element_type=F32)

    def proj_out(j, a, b):
        merged = (ga_ref[blk(j), :].astype(F32) * a + gb_ref[blk(j), :].astype(F32) * b)
        o_ref[blk(j), :] = x_ref[blk(j), :] + jnp.dot(
            merged.astype(BF16), wout_ref[...], preferred_element_type=F32)

    a_prev = None
    for j in range(nblk):
        s0 = scores(j, 0)
        s1 = scores(j, 1)
        spatial_gate(j)
        a_cur = proj_a(j)
        attend(j, 0, s0)
        if j > 0:
            b_prev = proj_b(j - 1)
        attend(j, 1, s1)
        if j > 0:
            proj_out(j - 1, a_prev, b_prev)
        a_prev = a_cur
    proj_out(nblk - 1, a_prev, proj_b(nblk - 1))


def _mixer(x2d, seq, u, v, q, k, va, ga, gb, sg_w, sg_bt, sink, w_a, w_b, w_out):
    n = x2d.shape[0]
    t = MIXER_TILE
    tiles_per_seq = seq // t
    blocks_per_tile = t // BLOCK
    blocks_per_seq = seq // BLOCK
    bsz = n // seq

    def tok(w):
        return pl.BlockSpec((t, w), lambda b, i: (b * tiles_per_seq + i, 0))

    def halo_prev(w):
        return pl.BlockSpec(
            (BLOCK, w),
            lambda b, i: (b * blocks_per_seq + jnp.maximum(i * blocks_per_tile - 1, 0), 0))

    def halo_next(w):
        return pl.BlockSpec(
            (BLOCK, w),
            lambda b, i: (b * blocks_per_seq
                          + jnp.minimum((i + 1) * blocks_per_tile, blocks_per_seq - 1), 0))

    return pl.pallas_call(
        _mixer_kernel,
        grid=(bsz, tiles_per_seq),
        in_specs=[pl.BlockSpec(memory_space=pltpu.SMEM),
                  tok(D_MODEL), tok(SG_WIDTH), tok(SG_WIDTH), tok(Q_W),
                  tok(KV_W), halo_prev(KV_W), halo_next(KV_W),
                  tok(KV_W), halo_prev(KV_W), halo_next(KV_W),
                  tok(D_MODEL), tok(D_MODEL),
                  _const_spec((SG_GROUPS, CHUNK, CHUNK)), _const_spec((CHUNK, SG_GROUPS)),
                  _const_spec((SG_WIDTH, D_MODEL)), _const_spec((Q_W, D_MODEL)),
                  _const_spec((D_MODEL, D_MODEL))],
        out_specs=tok(D_MODEL),
        scratch_shapes=[pltpu.VMEM((t + 2 * BLOCK, KV_W), BF16),
                        pltpu.VMEM((t + 2 * BLOCK, KV_W), BF16),
                        pltpu.VMEM((t, Q_W), BF16),
                        pltpu.VMEM((t, SG_WIDTH), BF16)],
        out_shape=jax.ShapeDtypeStruct((n, D_MODEL), F32),
        compiler_params=pltpu.CompilerParams(
            dimension_semantics=("parallel", "parallel"), vmem_limit_bytes=VMEM_LIMIT_BYTES),
        name="mixer",
    )(sink, x2d, u, v, q, k, k, k, va, va, va, ga, gb, sg_w, sg_bt, w_a, w_b, w_out)


def _ffn_kernel(x_ref, gffn_ref, wg_ref, wu_ref, wd_ref, gfin_ref, o_ref):
    x = x_ref[...]
    h = (x * _rms_scale(x) * gffn_ref[...]).astype(BF16)
    acc = x
    for lo, hi in zip(FF_BOUNDS[:-1], FF_BOUNDS[1:]):
        cols = slice(lo, hi)
        gate = jnp.dot(h, wg_ref[:, cols], preferred_element_type=F32)
        up = jnp.dot(h, wu_ref[:, cols], preferred_element_type=F32)
        act = (jax.nn.silu(gate) * up).astype(BF16)
        acc = acc + jnp.dot(act, wd_ref[cols, :], preferred_element_type=F32)
    o_ref[...] = acc * _rms_scale(acc) * gfin_ref[...]


def _ffn(x2d, g_ffn, w_gate, w_up, w_down, g_final):
    n = x2d.shape[0]
    t = FFN_TILE
    tok = pl.BlockSpec((t, D_MODEL), lambda i: (i, 0))
    return pl.pallas_call(
        _ffn_kernel,
        grid=(n // t,),
        in_specs=[tok, _const_spec((1, D_MODEL)), _const_spec((D_MODEL, D_FF)),
                  _const_spec((D_MODEL, D_FF)), _const_spec((D_FF, D_MODEL)),
                  _const_spec((1, D_MODEL))],
        out_specs=tok,
        out_shape=jax.ShapeDtypeStruct((n, D_MODEL), F32),
        compiler_params=pltpu.CompilerParams(
            dimension_semantics=("parallel",), vmem_limit_bytes=VMEM_LIMIT_BYTES),
        name="ffn",
    )(x2d, g_ffn, w_gate, w_up, w_down, g_final)


def _rope_tables(seq):
    half = HEAD_DIM // 2
    inv = ROPE_THETA ** (-jnp.arange(half, dtype=F32) / half)
    ang = jnp.arange(seq, dtype=F32)[:, None] * inv[None, :]
    cos = jnp.cos(ang)
    sin = jnp.sin(ang)
    return jnp.concatenate([cos, cos], axis=-1), jnp.concatenate([-sin, sin], axis=-1)


def _trunk(x, p):
    bsz, seq, _ = x.shape
    assert seq % INPROJ_TILE == 0 and seq % MIXER_TILE == 0 and (bsz * seq) % FFN_TILE == 0
    assert MIXER_TILE % BLOCK == 0 and BLOCK == CHUNK
    x2d = x.reshape(bsz * seq, D_MODEL)
    cos, sin = _rope_tables(seq)
    u, v, q, k, va, ga, gb = _inproj(x2d, seq, p["g_mix"], p["w_in"], p["ln_g"], p["ln_b"],
                                     cos, sin)
    x2d = _mixer(x2d, seq, u, v, q, k, va, ga, gb, p["sg_w"], p["sg_bt"], p["sink"],
                 p["w_a"], p["w_b"], p["w_out"])
    x2d = _ffn(x2d, p["g_ffn"], p["w_gate"], p["w_up"], p["w_down"], p["g_final"])
    return x2d.reshape(bsz, seq, D_MODEL)


def kernel(x_prompt, x_sample, g_mix, w_in, sg_ln_g, sg_ln_b, sg_w, sg_b, attn_sink,
           w_a, w_b, w_out, g_ffn, w_gate, w_up, w_down, g_final):
    assert w_in.shape[0] == 1, "single layer: the final rmsnorm is fused into its channel mixer"
    row = lambda a: a.reshape(1, -1).astype(F32)
    p = dict(
        g_mix=row(g_mix[0]), w_in=w_in[0].astype(BF16),
        ln_g=row(sg_ln_g[0]), ln_b=row(sg_ln_b[0]),
        sg_w=sg_w[0].astype(BF16), sg_bt=sg_b[0].T.astype(F32),
        sink=attn_sink[0].astype(F32),
        w_a=w_a[0].astype(BF16), w_b=w_b[0].astype(BF16), w_out=w_out[0].astype(BF16),
        g_ffn=row(g_ffn[0]), w_gate=w_gate[0].astype(BF16), w_up=w_up[0].astype(BF16),
        w_down=w_down[0].astype(BF16), g_final=row(g_final))
    return (_trunk(x_prompt, p), _trunk(x_sample, p))
```

```python
import jax
import jax.numpy as jnp
from jax import lax
from jax.experimental import pallas as pl
from jax.experimental.pallas import tpu as pltpu

D_MODEL = 1024
HEAD_DIM = 128
N_Q_HEADS = D_MODEL // HEAD_DIM
N_KV_HEADS = 2
Q_PER_KV = N_Q_HEADS // N_KV_HEADS
WINDOW = 128
BLOCK = 128
ROPE_THETA = 10000.0
SG_WIDTH = D_MODEL
SG_GROUPS = 4
SG_GROUP_DIM = SG_WIDTH // SG_GROUPS
CHUNK = 128
D_FF = -(-8 * D_MODEL // (3 * 256)) * 256
Q_W = N_Q_HEADS * HEAD_DIM
KV_W = N_KV_HEADS * HEAD_DIM
D_IN = 2 * SG_WIDTH + Q_W + 2 * KV_W + 2 * D_MODEL
RMS_EPS = 1e-6
LN_EPS = 1e-5
NEG = -1e30
LOG2_E = 1.4426950408889634

OFF_U = 0
OFF_V = OFF_U + SG_WIDTH
OFF_Q = OFF_V + SG_WIDTH
OFF_K = OFF_Q + Q_W
OFF_VA = OFF_K + KV_W
OFF_GA = OFF_VA + KV_W
OFF_GB = OFF_GA + D_MODEL

TOK_U = 0
TOK_V = TOK_U + SG_WIDTH
TOK_GA = TOK_V + SG_WIDTH
TOK_GB = TOK_GA + D_MODEL
TOK_K = TOK_GB + D_MODEL
TOK_W = TOK_K + KV_W
CH_W = Q_W + KV_W

V7X_VMEM_BYTES = 64 * 1024 * 1024
VMEM_LIMIT_BYTES = V7X_VMEM_BYTES - 8 * 1024 * 1024

INPROJ_TILE = 1024
INPROJ_SUB = 512
MIXER_TILE = 1024
MIXER_PROJ_BLOCKS = 2
FFN_TILE = 1024
FFN_SUB = 512
V7X_MXU_DIM = 256
FF_TILES = D_FF // V7X_MXU_DIM
FF_BOUNDS = (0, (FF_TILES // 2) * V7X_MXU_DIM, D_FF)

BF16 = jnp.bfloat16
F32 = jnp.float32

NT_DIMS = (((1,), (1,)), ((), ()))


def _const_spec(shape):
    nd = len(shape)
    return pl.BlockSpec(shape, lambda *_: (0,) * nd, pipeline_mode=pl.Buffered(1))


def _rms_scale(x):
    return lax.rsqrt(jnp.mean(x * x, axis=-1, keepdims=True) + RMS_EPS)


def _inproj_kernel(x_ref, gmix_ref, wtok_ref, wch_ref, lng_ref, lnb_ref,
                   cos_ref, sin_ref, cost_ref, sint_ref,
                   u_ref, v_ref, k_ref, ga_ref, gb_ref, qt_ref, vat_ref):
    half = HEAD_DIM // 2
    for r in range(INPROJ_TILE // INPROJ_SUB):
        rows = slice(r * INPROJ_SUB, (r + 1) * INPROJ_SUB)
        x = x_ref[rows, :]
        h = (x * _rms_scale(x) * gmix_ref[...]).astype(BF16)

        def proj(off, width):
            return jnp.dot(h, wtok_ref[:, off:off + width], preferred_element_type=F32)

        v = proj(TOK_V, SG_WIDTH)
        mu = jnp.mean(v, axis=-1, keepdims=True)
        vc = v - mu
        vn = vc * lax.rsqrt(jnp.mean(vc * vc, axis=-1, keepdims=True) + LN_EPS)
        v_ref[rows, :] = (vn * lng_ref[...] + lnb_ref[...]).astype(BF16)

        ga_ref[rows, :] = jax.nn.sigmoid(proj(TOK_GA, D_MODEL)).astype(BF16)

        zt = lax.dot_general(wch_ref[...], h, NT_DIMS, preferred_element_type=F32)
        scale = HEAD_DIM ** -0.5 * LOG2_E
        cq = cost_ref[:, rows] * scale
        sq = sint_ref[:, rows] * scale
        for hd in range(N_Q_HEADS):
            x1 = zt[hd * HEAD_DIM:hd * HEAD_DIM + half, :]
            x2 = zt[hd * HEAD_DIM + half:(hd + 1) * HEAD_DIM, :]
            qt_ref[hd * HEAD_DIM:hd * HEAD_DIM + half, rows] = (x1 * cq - x2 * sq).astype(BF16)
            qt_ref[hd * HEAD_DIM + half:(hd + 1) * HEAD_DIM, rows] = (
                x2 * cq + x1 * sq).astype(BF16)
        vat_ref[:, rows] = zt[Q_W:, :].astype(BF16)

        gb_ref[rows, :] = jax.nn.sigmoid(proj(TOK_GB, D_MODEL)).astype(BF16)

        u_ref[rows, :] = proj(TOK_U, SG_WIDTH).astype(BF16)

        cos = cos_ref[rows, :]
        sin = sin_ref[rows, :]
        k = proj(TOK_K, KV_W)
        for hd in range(N_KV_HEADS):
            sl = slice(hd * HEAD_DIM, (hd + 1) * HEAD_DIM)
            kh = k[:, sl]
            k_ref[rows, sl] = (kh * cos + pltpu.roll(kh, half, axis=1) * sin).astype(BF16)


def _inproj(x2d, seq, g_mix, w_tok, w_ch, ln_g, ln_b, rope):
    n = x2d.shape[0]
    t = INPROJ_TILE
    tiles_per_seq = seq // t
    tok = lambda w: pl.BlockSpec((t, w), lambda i: (i, 0))
    chan = lambda w: pl.BlockSpec((w, t), lambda i: (0, i))
    pos = pl.BlockSpec((t, HEAD_DIM), lambda i: (i % tiles_per_seq, 0))
    pos_t = pl.BlockSpec((HEAD_DIM // 2, t), lambda i: (0, i % tiles_per_seq))
    tok_widths = (SG_WIDTH, SG_WIDTH, KV_W, D_MODEL, D_MODEL)
    ch_widths = (Q_W, KV_W)
    return pl.pallas_call(
        _inproj_kernel,
        grid=(n // t,),
        in_specs=[tok(D_MODEL), _const_spec((1, D_MODEL)),
                  _const_spec((D_MODEL, TOK_W)), _const_spec((CH_W, D_MODEL)),
                  _const_spec((1, SG_WIDTH)), _const_spec((1, SG_WIDTH)),
                  pos, pos, pos_t, pos_t],
        out_specs=[tok(w) for w in tok_widths] + [chan(w) for w in ch_widths],
        out_shape=([jax.ShapeDtypeStruct((n, w), BF16) for w in tok_widths]
                   + [jax.ShapeDtypeStruct((w, n), BF16) for w in ch_widths]),
        compiler_params=pltpu.CompilerParams(
            dimension_semantics=("parallel",), vmem_limit_bytes=VMEM_LIMIT_BYTES),
        name="inproj",
    )(x2d, g_mix, w_tok, w_ch, ln_g, ln_b, *rope)


def _mixer_kernel(sink_ref, x_ref, u_ref, v_ref, qt_ref,
                  kc_ref, kp_ref, kn_ref, vtc_ref, vtp_ref, vtn_ref, ga_ref, gb_ref,
                  ws_ref, bs_ref, wa_ref, wb_ref, wout_ref, o_ref,
                  kfull, vtfull, attn_scr, sg_scr):
    t = MIXER_TILE
    nblk = t // BLOCK
    i = pl.program_id(1)
    first_tile = i == 0
    last_tile = i == pl.num_programs(1) - 1

    assert BLOCK == CHUNK

    def blk(j):
        return slice(j * BLOCK, (j + 1) * BLOCK)

    def spatial_gate(j):
        rows = blk(j)
        for g in range(SG_GROUPS):
            cols = slice(g * SG_GROUP_DIM, (g + 1) * SG_GROUP_DIM)
            mixed = jnp.dot(ws_ref[g], v_ref[rows, cols], preferred_element_type=F32)
            mixed = mixed + bs_ref[:, g:g + 1]
            sg_scr[rows, cols] = (u_ref[rows, cols].astype(F32) * mixed).astype(BF16)

    kfull[0:BLOCK, :] = kp_ref[...]
    kfull[BLOCK:BLOCK + t, :] = kc_ref[...]
    kfull[BLOCK + t:, :] = kn_ref[...]
    vtfull[:, 0:BLOCK] = vtp_ref[...]
    vtfull[:, BLOCK:BLOCK + t] = vtc_ref[...]
    vtfull[:, BLOCK + t:] = vtn_ref[...]

    lanes_q = Q_PER_KV * BLOCK
    kr = lax.broadcasted_iota(jnp.int32, (BLOCK, lanes_q), 0)
    qc = lax.broadcasted_iota(jnp.int32, (BLOCK, lanes_q), 1) % BLOCK
    prev_ok = kr >= qc
    next_ok = kr <= qc
    prev_ok_first = kr >= qc + jnp.where(first_tile, BLOCK, 0)
    next_ok_last = kr <= qc - jnp.where(last_tile, BLOCK, 0)

    def heads_of(hk):
        return [hk * Q_PER_KV + g for g in range(Q_PER_KV)]

    def scores(j, hk):
        kk = kfull[j * BLOCK:(j + 3) * BLOCK, hk * HEAD_DIM:(hk + 1) * HEAD_DIM]
        qt4 = jnp.concatenate(
            [qt_ref[hd * HEAD_DIM:(hd + 1) * HEAD_DIM, blk(j)] for hd in heads_of(hk)], axis=1)
        return jnp.dot(kk, qt4, preferred_element_type=F32)

    def attend(j, hk, s):
        p_ok = prev_ok_first if j == 0 else prev_ok
        n_ok = next_ok_last if j == nblk - 1 else next_ok
        heads = heads_of(hk)
        s_p = jnp.where(p_ok, s[:BLOCK], NEG)
        s_c = s[BLOCK:2 * BLOCK]
        s_n = jnp.where(n_ok, s[2 * BLOCK:], NEG)
        sink = jnp.concatenate(
            [jnp.full((1, BLOCK), sink_ref[hd] * LOG2_E, F32) for hd in heads], axis=1)
        m = jnp.maximum(jnp.maximum(s_p, s_c), s_n).max(axis=0, keepdims=True)
        m = jnp.maximum(m, sink)
        p_p = jnp.exp2(s_p - m)
        p_c = jnp.exp2(s_c - m)
        p_n = jnp.exp2(s_n - m)
        denom = (p_p + p_c + p_n).sum(axis=0, keepdims=True) + jnp.exp2(sink - m)
        p = jnp.concatenate([p_p, p_c, p_n], axis=0).astype(BF16)
        vt = vtfull[hk * HEAD_DIM:(hk + 1) * HEAD_DIM, j * BLOCK:(j + 3) * BLOCK]
        ot = jnp.dot(vt, p, preferred_element_type=F32) / denom
        for g, hd in enumerate(heads):
            attn_scr[blk(j), hd * HEAD_DIM:(hd + 1) * HEAD_DIM] = (
                ot[:, g * BLOCK:(g + 1) * BLOCK].T.astype(BF16))

    gb_ = MIXER_PROJ_BLOCKS
    ngrp = nblk // gb_

    def grp(g):
        return slice(g * gb_ * BLOCK, (g + 1) * gb_ * BLOCK)

    def proj_a(g):
        return jnp.dot(sg_scr[grp(g), :], wa_ref[...], preferred_element_type=F32)

    def proj_b(g):
        return jnp.dot(attn_scr[grp(g), :], wb_ref[...], preferred_element_type=F32)

    def proj_out(g, a, b):
        merged = (ga_ref[grp(g), :].astype(F32) * a + gb_ref[grp(g), :].astype(F32) * b)
        o_ref[grp(g), :] = x_ref[grp(g), :] + jnp.dot(
            merged.astype(BF16), wout_ref[...], preferred_element_type=F32)

    a_prev = b_prev = None
    for j in range(nblk):
        g, first_of_group, last_of_group = j // gb_, j % gb_ == 0, j % gb_ == gb_ - 1
        s0 = scores(j, 0)
        s1 = scores(j, 1)
        spatial_gate(j)
        if first_of_group and g > 0:
            b_prev = proj_b(g - 1)
        if last_of_group:
            a_cur = proj_a(g)
        attend(j, 0, s0)
        if first_of_group and g > 0:
            proj_out(g - 1, a_prev, b_prev)
        attend(j, 1, s1)
        if last_of_group:
            a_prev = a_cur
    proj_out(ngrp - 1, a_prev, proj_b(ngrp - 1))


def _mixer(x2d, seq, u, v, qt, k, vat, ga, gb, sg_w, sg_bt, sink, w_a, w_b, w_out):
    n = x2d.shape[0]
    t = MIXER_TILE
    tiles_per_seq = seq // t
    blocks_per_tile = t // BLOCK
    blocks_per_seq = seq // BLOCK
    bsz = n // seq

    def tile_idx(b, i):
        return b * tiles_per_seq + i

    def prev_idx(b, i):
        return b * blocks_per_seq + jnp.maximum(i * blocks_per_tile - 1, 0)

    def next_idx(b, i):
        return b * blocks_per_seq + jnp.minimum((i + 1) * blocks_per_tile, blocks_per_seq - 1)

    def tok(w):
        return pl.BlockSpec((t, w), lambda b, i: (tile_idx(b, i), 0))

    return pl.pallas_call(
        _mixer_kernel,
        grid=(bsz, tiles_per_seq),
        in_specs=[pl.BlockSpec(memory_space=pltpu.SMEM),
                  tok(D_MODEL), tok(SG_WIDTH), tok(SG_WIDTH),
                  pl.BlockSpec((Q_W, t), lambda b, i: (0, tile_idx(b, i))),
                  tok(KV_W),
                  pl.BlockSpec((BLOCK, KV_W), lambda b, i: (prev_idx(b, i), 0)),
                  pl.BlockSpec((BLOCK, KV_W), lambda b, i: (next_idx(b, i), 0)),
                  pl.BlockSpec((KV_W, t), lambda b, i: (0, tile_idx(b, i))),
                  pl.BlockSpec((KV_W, BLOCK), lambda b, i: (0, prev_idx(b, i))),
                  pl.BlockSpec((KV_W, BLOCK), lambda b, i: (0, next_idx(b, i))),
                  tok(D_MODEL), tok(D_MODEL),
                  _const_spec((SG_GROUPS, CHUNK, CHUNK)), _const_spec((CHUNK, SG_GROUPS)),
                  _const_spec((SG_WIDTH, D_MODEL)), _const_spec((Q_W, D_MODEL)),
                  _const_spec((D_MODEL, D_MODEL))],
        out_specs=tok(D_MODEL),
        scratch_shapes=[pltpu.VMEM((t + 2 * BLOCK, KV_W), BF16),
                        pltpu.VMEM((KV_W, t + 2 * BLOCK), BF16),
                        pltpu.VMEM((t, Q_W), BF16),
                        pltpu.VMEM((t, SG_WIDTH), BF16)],
        out_shape=jax.ShapeDtypeStruct((n, D_MODEL), F32),
        compiler_params=pltpu.CompilerParams(
            dimension_semantics=("parallel", "parallel"), vmem_limit_bytes=VMEM_LIMIT_BYTES),
        name="mixer",
    )(sink, x2d, u, v, qt, k, k, k, vat, vat, vat, ga, gb, sg_w, sg_bt, w_a, w_b, w_out)


def _ffn_kernel(x_ref, gffn_ref, wg_ref, wu_ref, wd_ref, gfin_ref, o_ref):
    for r in range(FFN_TILE // FFN_SUB):
        rows = slice(r * FFN_SUB, (r + 1) * FFN_SUB)
        x = x_ref[rows, :]
        h = (x * _rms_scale(x) * gffn_ref[...]).astype(BF16)
        acc = x
        for lo, hi in zip(FF_BOUNDS[:-1], FF_BOUNDS[1:]):
            cols = slice(lo, hi)
            gate = jnp.dot(h, wg_ref[:, cols], preferred_element_type=F32)
            up = jnp.dot(h, wu_ref[:, cols], preferred_element_type=F32)
            act = (jax.nn.silu(gate) * up).astype(BF16)
            acc = acc + jnp.dot(act, wd_ref[cols, :], preferred_element_type=F32)
        o_ref[rows, :] = acc * _rms_scale(acc) * gfin_ref[...]


def _ffn(x2d, g_ffn, w_gate, w_up, w_down, g_final):
    n = x2d.shape[0]
    t = FFN_TILE
    tok = pl.BlockSpec((t, D_MODEL), lambda i: (i, 0))
    return pl.pallas_call(
        _ffn_kernel,
        grid=(n // t,),
        in_specs=[tok, _const_spec((1, D_MODEL)), _const_spec((D_MODEL, D_FF)),
                  _const_spec((D_MODEL, D_FF)), _const_spec((D_FF, D_MODEL)),
                  _const_spec((1, D_MODEL))],
        out_specs=tok,
        out_shape=jax.ShapeDtypeStruct((n, D_MODEL), F32),
        compiler_params=pltpu.CompilerParams(
            dimension_semantics=("parallel",), vmem_limit_bytes=VMEM_LIMIT_BYTES),
        name="ffn",
    )(x2d, g_ffn, w_gate, w_up, w_down, g_final)


def _rope_tables(seq):
    half = HEAD_DIM // 2
    inv = ROPE_THETA ** (-jnp.arange(half, dtype=F32) / half)
    ang = jnp.arange(seq, dtype=F32)[:, None] * inv[None, :]
    cos = jnp.cos(ang)
    sin = jnp.sin(ang)
    return (jnp.concatenate([cos, cos], axis=-1), jnp.concatenate([-sin, sin], axis=-1),
            cos.T, sin.T)


def _trunk(x, p, rope):
    bsz, seq, _ = x.shape
    assert seq % INPROJ_TILE == 0 and seq % MIXER_TILE == 0 and (bsz * seq) % FFN_TILE == 0
    assert MIXER_TILE % BLOCK == 0 and BLOCK == CHUNK
    x2d = x.reshape(bsz * seq, D_MODEL)
    u, v, k, ga, gb, qt, vat = _inproj(x2d, seq, p["g_mix"], p["w_tok"], p["w_ch"],
                                       p["ln_g"], p["ln_b"], rope)
    x2d = _mixer(x2d, seq, u, v, qt, k, vat, ga, gb, p["sg_w"], p["sg_bt"], p["sink"],
                 p["w_a"], p["w_b"], p["w_out"])
    x2d = _ffn(x2d, p["g_ffn"], p["w_gate"], p["w_up"], p["w_down"], p["g_final"])
    return x2d.reshape(bsz, seq, D_MODEL)


def kernel(x_prompt, x_sample, g_mix, w_in, sg_ln_g, sg_ln_b, sg_w, sg_b, attn_sink,
           w_a, w_b, w_out, g_ffn, w_gate, w_up, w_down, g_final):
    assert w_in.shape[0] == 1, "single layer: the final rmsnorm is fused into its channel mixer"
    row = lambda a: a.reshape(1, -1).astype(F32)
    w = w_in[0].astype(BF16)
    seg = lambda off, width: w[:, off:off + width]
    w_tok = jnp.concatenate(
        [seg(OFF_U, SG_WIDTH), seg(OFF_V, SG_WIDTH), seg(OFF_GA, D_MODEL), seg(OFF_GB, D_MODEL),
         seg(OFF_K, KV_W)], axis=1)
    w_ch = jnp.concatenate([seg(OFF_Q, Q_W), seg(OFF_VA, KV_W)], axis=1).T
    p = dict(
        g_mix=row(g_mix[0]), w_tok=w_tok, w_ch=w_ch,
        ln_g=row(sg_ln_g[0]), ln_b=row(sg_ln_b[0]),
        sg_w=sg_w[0].astype(BF16), sg_bt=sg_b[0].T.astype(F32),
        sink=attn_sink[0].astype(F32),
        w_a=w_a[0].astype(BF16), w_b=w_b[0].astype(BF16), w_out=w_out[0].astype(BF16),
        g_ffn=row(g_ffn[0]), w_gate=w_gate[0].astype(BF16), w_up=w_up[0].astype(BF16),
        w_down=w_down[0].astype(BF16), g_final=row(g_final))
    rope = _rope_tables(max(x_prompt.shape[1], x_sample.shape[1]))
    return (_trunk(x_prompt, p, rope), _trunk(x_sample, p, rope))
```

```python
import jax
import jax.numpy as jnp
from jax import lax
from jax.experimental import pallas as pl
from jax.experimental.pallas import tpu as pltpu

D_MODEL = 1024
HEAD_DIM = 128
N_Q_HEADS = D_MODEL // HEAD_DIM
N_KV_HEADS = 2
Q_PER_KV = N_Q_HEADS // N_KV_HEADS
WINDOW = 128
BLOCK = 128
ROPE_THETA = 10000.0
SG_WIDTH = D_MODEL
SG_GROUPS = 4
SG_GROUP_DIM = SG_WIDTH // SG_GROUPS
CHUNK = 128
D_FF = -(-8 * D_MODEL // (3 * 256)) * 256
Q_W = N_Q_HEADS * HEAD_DIM
KV_W = N_KV_HEADS * HEAD_DIM
D_IN = 2 * SG_WIDTH + Q_W + 2 * KV_W + 2 * D_MODEL
RMS_EPS = 1e-6
LN_EPS = 1e-5
NEG = -1e30
LOG2_E = 1.4426950408889634

OFF_U = 0
OFF_V = OFF_U + SG_WIDTH
OFF_Q = OFF_V + SG_WIDTH
OFF_K = OFF_Q + Q_W
OFF_VA = OFF_K + KV_W
OFF_GA = OFF_VA + KV_W
OFF_GB = OFF_GA + D_MODEL

CH_W = Q_W + KV_W

V7X_VMEM_BYTES = 64 * 1024 * 1024
VMEM_LIMIT_BYTES = V7X_VMEM_BYTES - 8 * 1024 * 1024

INPROJ_TILE = 1024
INPROJ_SUB = 256
MIXER_TILE = 1024
MIXER_PROJ_BLOCKS = 2
FFN_TILE = 1024
FFN_SUB = 256
ROPE_LO = 64
V7X_MXU_DIM = 256
FF_TILES = D_FF // V7X_MXU_DIM
FF_BOUNDS = (0, (FF_TILES // 2) * V7X_MXU_DIM, D_FF)

BF16 = jnp.bfloat16
F32 = jnp.float32

NT_DIMS = (((1,), (1,)), ((), ()))


def _const_spec(shape):
    nd = len(shape)
    return pl.BlockSpec(shape, lambda *_: (0,) * nd, pipeline_mode=pl.Buffered(1))


def _rms_scale(x):
    return lax.rsqrt(jnp.mean(x * x, axis=-1, keepdims=True) + RMS_EPS)


def _inproj_kernel(x_ref, gmix_ref, wtok_ref, wch_ref, lng_ref, lnb_ref,
                   cos_ref, sin_ref, cost_ref, sint_ref,
                   u_ref, v_ref, k_ref, ga_ref, gb_ref, qt_ref, vat_ref):
    half = HEAD_DIM // 2
    for r in range(INPROJ_TILE // INPROJ_SUB):
        rows = slice(r * INPROJ_SUB, (r + 1) * INPROJ_SUB)
        x = x_ref[rows, :]
        h = (x * _rms_scale(x) * gmix_ref[...]).astype(BF16)

        def proj(off, width):
            return jnp.dot(h, wtok_ref[:, off:off + width], preferred_element_type=F32)

        v = proj(OFF_V, SG_WIDTH)
        mu = jnp.mean(v, axis=-1, keepdims=True)
        vc = v - mu
        vn = vc * lax.rsqrt(jnp.mean(vc * vc, axis=-1, keepdims=True) + LN_EPS)
        v_ref[rows, :] = (vn * lng_ref[...] + lnb_ref[...]).astype(BF16)

        ga_ref[rows, :] = jax.nn.sigmoid(proj(OFF_GA, D_MODEL)).astype(BF16)

        zt = lax.dot_general(wch_ref[...], h, NT_DIMS, preferred_element_type=F32)
        scale = HEAD_DIM ** -0.5 * LOG2_E
        cq = cost_ref[:, rows] * scale
        sq = sint_ref[:, rows] * scale
        for hd in range(N_Q_HEADS):
            x1 = zt[hd * HEAD_DIM:hd * HEAD_DIM + half, :]
            x2 = zt[hd * HEAD_DIM + half:(hd + 1) * HEAD_DIM, :]
            qt_ref[hd * HEAD_DIM:hd * HEAD_DIM + half, rows] = (x1 * cq - x2 * sq).astype(BF16)
            qt_ref[hd * HEAD_DIM + half:(hd + 1) * HEAD_DIM, rows] = (
                x2 * cq + x1 * sq).astype(BF16)
        vat_ref[:, rows] = zt[Q_W:, :].astype(BF16)

        gb_ref[rows, :] = jax.nn.sigmoid(proj(OFF_GB, D_MODEL)).astype(BF16)

        u_ref[rows, :] = proj(OFF_U, SG_WIDTH).astype(BF16)

        cos = cos_ref[rows, :]
        sin = sin_ref[rows, :]
        k = proj(OFF_K, KV_W)
        for hd in range(N_KV_HEADS):
            sl = slice(hd * HEAD_DIM, (hd + 1) * HEAD_DIM)
            kh = k[:, sl]
            k_ref[rows, sl] = (kh * cos + pltpu.roll(kh, half, axis=1) * sin).astype(BF16)


def _inproj(x2d, seq, g_mix, w_tok, w_ch, ln_g, ln_b, rope):
    n = x2d.shape[0]
    t = INPROJ_TILE
    tiles_per_seq = seq // t
    tok = lambda w: pl.BlockSpec((t, w), lambda i: (i, 0))
    chan = lambda w: pl.BlockSpec((w, t), lambda i: (0, i))
    pos = pl.BlockSpec((t, HEAD_DIM), lambda i: (i % tiles_per_seq, 0))
    pos_t = pl.BlockSpec((HEAD_DIM // 2, t), lambda i: (0, i % tiles_per_seq))
    tok_widths = (SG_WIDTH, SG_WIDTH, KV_W, D_MODEL, D_MODEL)
    ch_widths = (Q_W, KV_W)
    return pl.pallas_call(
        _inproj_kernel,
        grid=(n // t,),
        in_specs=[tok(D_MODEL), _const_spec((1, D_MODEL)),
                  _const_spec((D_MODEL, D_IN)), _const_spec((CH_W, D_MODEL)),
                  _const_spec((1, SG_WIDTH)), _const_spec((1, SG_WIDTH)),
                  pos, pos, pos_t, pos_t],
        out_specs=[tok(w) for w in tok_widths] + [chan(w) for w in ch_widths],
        out_shape=([jax.ShapeDtypeStruct((n, w), BF16) for w in tok_widths]
                   + [jax.ShapeDtypeStruct((w, n), BF16) for w in ch_widths]),
        compiler_params=pltpu.CompilerParams(
            dimension_semantics=("parallel",), vmem_limit_bytes=VMEM_LIMIT_BYTES),
        name="inproj",
    )(x2d, g_mix, w_tok, w_ch, ln_g, ln_b, *rope)


def _mixer_kernel(sink_ref, x_ref, u_ref, v_ref, qt_ref,
                  kc_ref, kp_ref, kn_ref, vtc_ref, vtp_ref, vtn_ref, ga_ref, gb_ref,
                  ws_ref, bs_ref, wa_ref, wb_ref, wout_ref, o_ref,
                  kfull, vtfull, attn_scr, sg_scr):
    t = MIXER_TILE
    nblk = t // BLOCK
    i = pl.program_id(1)
    first_tile = i == 0
    last_tile = i == pl.num_programs(1) - 1

    assert BLOCK == CHUNK

    def blk(j):
        return slice(j * BLOCK, (j + 1) * BLOCK)

    def spatial_gate(j):
        rows = blk(j)
        for g in range(SG_GROUPS):
            cols = slice(g * SG_GROUP_DIM, (g + 1) * SG_GROUP_DIM)
            mixed = jnp.dot(ws_ref[g], v_ref[rows, cols], preferred_element_type=F32)
            mixed = mixed + bs_ref[:, g:g + 1]
            sg_scr[rows, cols] = (u_ref[rows, cols].astype(F32) * mixed).astype(BF16)

    kfull[0:BLOCK, :] = kp_ref[...]
    kfull[BLOCK:BLOCK + t, :] = kc_ref[...]
    kfull[BLOCK + t:, :] = kn_ref[...]
    vtfull[:, 0:BLOCK] = vtp_ref[...]
    vtfull[:, BLOCK:BLOCK + t] = vtc_ref[...]
    vtfull[:, BLOCK + t:] = vtn_ref[...]

    lanes_q = Q_PER_KV * BLOCK
    kr = lax.broadcasted_iota(jnp.int32, (BLOCK, lanes_q), 0)
    qc = lax.broadcasted_iota(jnp.int32, (BLOCK, lanes_q), 1) % BLOCK
    prev_ok = kr >= qc
    next_ok = kr <= qc
    prev_ok_first = kr >= qc + jnp.where(first_tile, BLOCK, 0)
    next_ok_last = kr <= qc - jnp.where(last_tile, BLOCK, 0)

    def heads_of(hk):
        return [hk * Q_PER_KV + g for g in range(Q_PER_KV)]

    def scores(j, hk):
        kk = kfull[j * BLOCK:(j + 3) * BLOCK, hk * HEAD_DIM:(hk + 1) * HEAD_DIM]
        qt4 = jnp.concatenate(
            [qt_ref[hd * HEAD_DIM:(hd + 1) * HEAD_DIM, blk(j)] for hd in heads_of(hk)], axis=1)
        return jnp.dot(kk, qt4, preferred_element_type=F32)

    def attend(j, hk, s):
        p_ok = prev_ok_first if j == 0 else prev_ok
        n_ok = next_ok_last if j == nblk - 1 else next_ok
        heads = heads_of(hk)
        s_p = jnp.where(p_ok, s[:BLOCK], NEG)
        s_c = s[BLOCK:2 * BLOCK]
        s_n = jnp.where(n_ok, s[2 * BLOCK:], NEG)
        sink = jnp.concatenate(
            [jnp.full((1, BLOCK), sink_ref[hd] * LOG2_E, F32) for hd in heads], axis=1)
        m = jnp.maximum(jnp.maximum(s_p, s_c), s_n).max(axis=0, keepdims=True)
        m = jnp.maximum(m, sink)
        p_p = jnp.exp2(s_p - m)
        p_c = jnp.exp2(s_c - m)
        p_n = jnp.exp2(s_n - m)
        denom = (p_p + p_c + p_n).sum(axis=0, keepdims=True) + jnp.exp2(sink - m)
        p = jnp.concatenate([p_p, p_c, p_n], axis=0).astype(BF16)
        vt = vtfull[hk * HEAD_DIM:(hk + 1) * HEAD_DIM, j * BLOCK:(j + 3) * BLOCK]
        ot = jnp.dot(vt, p, preferred_element_type=F32) / denom
        for g, hd in enumerate(heads):
            attn_scr[blk(j), hd * HEAD_DIM:(hd + 1) * HEAD_DIM] = (
                ot[:, g * BLOCK:(g + 1) * BLOCK].T.astype(BF16))

    gb_ = MIXER_PROJ_BLOCKS
    ngrp = nblk // gb_

    def grp(g):
        return slice(g * gb_ * BLOCK, (g + 1) * gb_ * BLOCK)

    def proj_a(g):
        return jnp.dot(sg_scr[grp(g), :], wa_ref[...], preferred_element_type=F32)

    def proj_b(g):
        return jnp.dot(attn_scr[grp(g), :], wb_ref[...], preferred_element_type=F32)

    def proj_out(g, a, b):
        merged = (ga_ref[grp(g), :].astype(F32) * a + gb_ref[grp(g), :].astype(F32) * b)
        o_ref[grp(g), :] = x_ref[grp(g), :] + jnp.dot(
            merged.astype(BF16), wout_ref[...], preferred_element_type=F32)

    a_prev = b_prev = None
    for j in range(nblk):
        g, first_of_group, last_of_group = j // gb_, j % gb_ == 0, j % gb_ == gb_ - 1
        s0 = scores(j, 0)
        s1 = scores(j, 1)
        spatial_gate(j)
        if first_of_group and g > 0:
            b_prev = proj_b(g - 1)
        if last_of_group:
            a_cur = proj_a(g)
        attend(j, 0, s0)
        if first_of_group and g > 0:
            proj_out(g - 1, a_prev, b_prev)
        attend(j, 1, s1)
        if last_of_group:
            a_prev = a_cur
    proj_out(ngrp - 1, a_prev, proj_b(ngrp - 1))


def _mixer(x2d, seq, u, v, qt, k, vat, ga, gb, sg_w, sg_bt, sink, w_a, w_b, w_out):
    n = x2d.shape[0]
    t = MIXER_TILE
    tiles_per_seq = seq // t
    blocks_per_tile = t // BLOCK
    blocks_per_seq = seq // BLOCK
    bsz = n // seq

    def tile_idx(b, i):
        return b * tiles_per_seq + i

    def prev_idx(b, i):
        return b * blocks_per_seq + jnp.maximum(i * blocks_per_tile - 1, 0)

    def next_idx(b, i):
        return b * blocks_per_seq + jnp.minimum((i + 1) * blocks_per_tile, blocks_per_seq - 1)

    def tok(w):
        return pl.BlockSpec((t, w), lambda b, i: (tile_idx(b, i), 0))

    return pl.pallas_call(
        _mixer_kernel,
        grid=(bsz, tiles_per_seq),
        in_specs=[pl.BlockSpec(memory_space=pltpu.SMEM),
                  tok(D_MODEL), tok(SG_WIDTH), tok(SG_WIDTH),
                  pl.BlockSpec((Q_W, t), lambda b, i: (0, tile_idx(b, i))),
                  tok(KV_W),
                  pl.BlockSpec((BLOCK, KV_W), lambda b, i: (prev_idx(b, i), 0)),
                  pl.BlockSpec((BLOCK, KV_W), lambda b, i: (next_idx(b, i), 0)),
                  pl.BlockSpec((KV_W, t), lambda b, i: (0, tile_idx(b, i))),
                  pl.BlockSpec((KV_W, BLOCK), lambda b, i: (0, prev_idx(b, i))),
                  pl.BlockSpec((KV_W, BLOCK), lambda b, i: (0, next_idx(b, i))),
                  tok(D_MODEL), tok(D_MODEL),
                  _const_spec((SG_GROUPS, CHUNK, CHUNK)), _const_spec((CHUNK, SG_GROUPS)),
                  _const_spec((SG_WIDTH, D_MODEL)), _const_spec((Q_W, D_MODEL)),
                  _const_spec((D_MODEL, D_MODEL))],
        out_specs=tok(D_MODEL),
        scratch_shapes=[pltpu.VMEM((t + 2 * BLOCK, KV_W), BF16),
                        pltpu.VMEM((KV_W, t + 2 * BLOCK), BF16),
                        pltpu.VMEM((t, Q_W), BF16),
                        pltpu.VMEM((t, SG_WIDTH), BF16)],
        out_shape=jax.ShapeDtypeStruct((n, D_MODEL), F32),
        compiler_params=pltpu.CompilerParams(
            dimension_semantics=("parallel", "parallel"), vmem_limit_bytes=VMEM_LIMIT_BYTES),
        name="mixer",
    )(sink, x2d, u, v, qt, k, k, k, vat, vat, vat, ga, gb, sg_w, sg_bt, w_a, w_b, w_out)


def _ffn_kernel(x_ref, gffn_ref, wg_ref, wu_ref, wd_ref, gfin_ref, o_ref):
    n_pass = len(FF_BOUNDS) - 1
    stages = [(r, c) for r in range(FFN_TILE // FFN_SUB) for c in range(n_pass)]
    xs, hs, accs = {}, {}, {}

    def gate_up(r, c):
        rows = slice(r * FFN_SUB, (r + 1) * FFN_SUB)
        if c == 0:
            xs[r] = x_ref[rows, :]
            hs[r] = (xs[r] * _rms_scale(xs[r]) * gffn_ref[...]).astype(BF16)
            accs[r] = xs[r]
        cols = slice(FF_BOUNDS[c], FF_BOUNDS[c + 1])
        gate = jnp.dot(hs[r], wg_ref[:, cols], preferred_element_type=F32)
        up = jnp.dot(hs[r], wu_ref[:, cols], preferred_element_type=F32)
        return (jax.nn.silu(gate) * up).astype(BF16)

    def down(r, c, act):
        rows = slice(r * FFN_SUB, (r + 1) * FFN_SUB)
        cols = slice(FF_BOUNDS[c], FF_BOUNDS[c + 1])
        accs[r] = accs[r] + jnp.dot(act, wd_ref[cols, :], preferred_element_type=F32)
        if c == n_pass - 1:
            o_ref[rows, :] = accs[r] * _rms_scale(accs[r]) * gfin_ref[...]

    act = gate_up(*stages[0])
    for s, stage in enumerate(stages):
        act_next = gate_up(*stages[s + 1]) if s + 1 < len(stages) else None
        down(*stage, act)
        act = act_next


def _ffn(x2d, g_ffn, w_gate, w_up, w_down, g_final):
    n = x2d.shape[0]
    t = FFN_TILE
    tok = pl.BlockSpec((t, D_MODEL), lambda i: (i, 0))
    return pl.pallas_call(
        _ffn_kernel,
        grid=(n // t,),
        in_specs=[tok, _const_spec((1, D_MODEL)), _const_spec((D_MODEL, D_FF)),
                  _const_spec((D_MODEL, D_FF)), _const_spec((D_FF, D_MODEL)),
                  _const_spec((1, D_MODEL))],
        out_specs=tok,
        out_shape=jax.ShapeDtypeStruct((n, D_MODEL), F32),
        compiler_params=pltpu.CompilerParams(
            dimension_semantics=("parallel",), vmem_limit_bytes=VMEM_LIMIT_BYTES),
        name="ffn",
    )(x2d, g_ffn, w_gate, w_up, w_down, g_final)


def _rope_tables(seq):
    half = HEAD_DIM // 2
    assert seq % ROPE_LO == 0
    inv = ROPE_THETA ** (-jnp.arange(half, dtype=F32) / half)
    a_hi = (jnp.arange(seq // ROPE_LO, dtype=F32) * ROPE_LO)[:, None] * inv[None, :]
    a_lo = jnp.arange(ROPE_LO, dtype=F32)[:, None] * inv[None, :]
    ch, sh = jnp.cos(a_hi)[:, None, :], jnp.sin(a_hi)[:, None, :]
    cl, sl = jnp.cos(a_lo)[None], jnp.sin(a_lo)[None]
    cos = (ch * cl - sh * sl).reshape(seq, half)
    sin = (sh * cl + ch * sl).reshape(seq, half)
    return (jnp.concatenate([cos, cos], axis=-1), jnp.concatenate([-sin, sin], axis=-1),
            cos.T, sin.T)


def _trunk(x, p, rope):
    bsz, seq, _ = x.shape
    assert seq % INPROJ_TILE == 0 and seq % MIXER_TILE == 0 and (bsz * seq) % FFN_TILE == 0
    assert MIXER_TILE % BLOCK == 0 and BLOCK == CHUNK
    x2d = x.reshape(bsz * seq, D_MODEL)
    u, v, k, ga, gb, qt, vat = _inproj(x2d, seq, p["g_mix"], p["w_tok"], p["w_ch"],
                                       p["ln_g"], p["ln_b"], rope)
    x2d = _mixer(x2d, seq, u, v, qt, k, vat, ga, gb, p["sg_w"], p["sg_bt"], p["sink"],
                 p["w_a"], p["w_b"], p["w_out"])
    x2d = _ffn(x2d, p["g_ffn"], p["w_gate"], p["w_up"], p["w_down"], p["g_final"])
    return x2d.reshape(bsz, seq, D_MODEL)


def kernel(x_prompt, x_sample, g_mix, w_in, sg_ln_g, sg_ln_b, sg_w, sg_b, attn_sink,
           w_a, w_b, w_out, g_ffn, w_gate, w_up, w_down, g_final):
    assert w_in.shape[0] == 1, "single layer: the final rmsnorm is fused into its channel mixer"
    row = lambda a: a.reshape(1, -1).astype(F32)
    w = w_in[0]
    w_ch = jnp.concatenate(
        [w[:, OFF_Q:OFF_Q + Q_W], w[:, OFF_VA:OFF_VA + KV_W]], axis=1).T.astype(BF16)
    p = dict(
        g_mix=row(g_mix[0]), w_tok=w.astype(BF16), w_ch=w_ch,
        ln_g=row(sg_ln_g[0]), ln_b=row(sg_ln_b[0]),
        sg_w=sg_w[0].astype(BF16), sg_bt=sg_b[0].T.astype(F32),
        sink=attn_sink[0].astype(F32),
        w_a=w_a[0].astype(BF16), w_b=w_b[0].astype(BF16), w_out=w_out[0].astype(BF16),
        g_ffn=row(g_ffn[0]), w_gate=w_gate[0].astype(BF16), w_up=w_up[0].astype(BF16),
        w_down=w_down[0].astype(BF16), g_final=row(g_final))
    rope = _rope_tables(max(x_prompt.shape[1], x_sample.shape[1]))
    return (_trunk(x_prompt, p, rope), _trunk(x_sample, p, rope))
```

```python
import jax
import jax.numpy as jnp
from jax import lax
from jax.experimental import pallas as pl
from jax.experimental.pallas import tpu as pltpu

D_MODEL = 1024
HEAD_DIM = 128
N_Q_HEADS = D_MODEL // HEAD_DIM
N_KV_HEADS = 2
Q_PER_KV = N_Q_HEADS // N_KV_HEADS
WINDOW = 128
BLOCK = 128
ROPE_THETA = 10000.0
SG_WIDTH = D_MODEL
SG_GROUPS = 4
SG_GROUP_DIM = SG_WIDTH // SG_GROUPS
CHUNK = 128
D_FF = -(-8 * D_MODEL // (3 * 256)) * 256
Q_W = N_Q_HEADS * HEAD_DIM
KV_W = N_KV_HEADS * HEAD_DIM
D_IN = 2 * SG_WIDTH + Q_W + 2 * KV_W + 2 * D_MODEL
RMS_EPS = 1e-6
LN_EPS = 1e-5
NEG = -1e30
LOG2_E = 1.4426950408889634

OFF_U = 0
OFF_V = OFF_U + SG_WIDTH
OFF_Q = OFF_V + SG_WIDTH
OFF_K = OFF_Q + Q_W
OFF_VA = OFF_K + KV_W
OFF_GA = OFF_VA + KV_W
OFF_GB = OFF_GA + D_MODEL

CH_W = Q_W + KV_W

V7X_VMEM_BYTES = 64 * 1024 * 1024
VMEM_LIMIT_BYTES = V7X_VMEM_BYTES - 8 * 1024 * 1024

INPROJ_TILE = 1024
INPROJ_SUB = 256
MIXER_TILE = 1024
ONES_ROWS = 16
MIXER_PROJ_BLOCKS = 2
FFN_TILE = 1024
FFN_SUB = 256
ROPE_LO = 64
V7X_MXU_DIM = 256
FF_TILES = D_FF // V7X_MXU_DIM
FF_BOUNDS = (0, (FF_TILES // 2) * V7X_MXU_DIM, D_FF)

BF16 = jnp.bfloat16
F32 = jnp.float32

NT_DIMS = (((1,), (1,)), ((), ()))


def _const_spec(shape):
    nd = len(shape)
    return pl.BlockSpec(shape, lambda *_: (0,) * nd, pipeline_mode=pl.Buffered(1))


def _rms_scale(x):
    return lax.rsqrt(jnp.mean(x * x, axis=-1, keepdims=True) + RMS_EPS)


def _wt_kernel(w_ref, o_ref):
    o_ref[...] = w_ref[...].T.astype(BF16)


def _channel_major_weights(w):
    cb = KV_W
    assert OFF_Q % cb == 0 and OFF_VA % cb == 0 and Q_W % cb == 0
    n_q = Q_W // cb
    return pl.pallas_call(
        _wt_kernel,
        grid=(CH_W // cb,),
        in_specs=[pl.BlockSpec(
            (D_MODEL, cb), lambda i: (0, jnp.where(i < n_q, OFF_Q // cb + i, OFF_VA // cb)))],
        out_specs=pl.BlockSpec((cb, D_MODEL), lambda i: (i, 0)),
        out_shape=jax.ShapeDtypeStruct((CH_W, D_MODEL), BF16),
        compiler_params=pltpu.CompilerParams(dimension_semantics=("parallel",)),
        name="w_in_channel_major",
    )(w)


def _inproj_kernel(x_ref, gmix_ref, wtok_ref, wch_ref, lng_ref, lnb_ref,
                   cos_ref, sin_ref,
                   u_ref, v_ref, k_ref, ga_ref, gb_ref, qt_ref, vat_ref):
    half = HEAD_DIM // 2
    for r in range(INPROJ_TILE // INPROJ_SUB):
        rows = slice(r * INPROJ_SUB, (r + 1) * INPROJ_SUB)
        x = x_ref[rows, :]
        h = (x * _rms_scale(x) * gmix_ref[...]).astype(BF16)

        def proj(off, width):
            return jnp.dot(h, wtok_ref[:, off:off + width], preferred_element_type=F32)

        v = proj(OFF_V, SG_WIDTH)
        mu = jnp.mean(v, axis=-1, keepdims=True)
        vc = v - mu
        vn = vc * lax.rsqrt(jnp.mean(vc * vc, axis=-1, keepdims=True) + LN_EPS)
        v_ref[rows, :] = (vn * lng_ref[...] + lnb_ref[...]).astype(BF16)

        ga_ref[rows, :] = jax.nn.sigmoid(proj(OFF_GA, D_MODEL)).astype(BF16)

        zt = lax.dot_general(wch_ref[...], h, NT_DIMS, preferred_element_type=F32)
        scale = HEAD_DIM ** -0.5 * LOG2_E
        cq = cos_ref[rows, :].T[:half] * scale
        sq = sin_ref[rows, :].T[half:] * scale
        for hd in range(N_Q_HEADS):
            x1 = zt[hd * HEAD_DIM:hd * HEAD_DIM + half, :]
            x2 = zt[hd * HEAD_DIM + half:(hd + 1) * HEAD_DIM, :]
            qt_ref[hd * HEAD_DIM:hd * HEAD_DIM + half, rows] = (x1 * cq - x2 * sq).astype(BF16)
            qt_ref[hd * HEAD_DIM + half:(hd + 1) * HEAD_DIM, rows] = (
                x2 * cq + x1 * sq).astype(BF16)
        vat_ref[:, rows] = zt[Q_W:, :].astype(BF16)

        gb_ref[rows, :] = jax.nn.sigmoid(proj(OFF_GB, D_MODEL)).astype(BF16)

        u_ref[rows, :] = proj(OFF_U, SG_WIDTH).astype(BF16)

        cos = cos_ref[rows, :]
        sin = sin_ref[rows, :]
        k = proj(OFF_K, KV_W)
        for hd in range(N_KV_HEADS):
            sl = slice(hd * HEAD_DIM, (hd + 1) * HEAD_DIM)
            kh = k[:, sl]
            k_ref[rows, sl] = (kh * cos + pltpu.roll(kh, half, axis=1) * sin).astype(BF16)


def _inproj(x2d, seq, g_mix, w_tok, w_ch, ln_g, ln_b, rope):
    n = x2d.shape[0]
    t = INPROJ_TILE
    tiles_per_seq = seq // t
    tok = lambda w: pl.BlockSpec((t, w), lambda i: (i, 0))
    chan = lambda w: pl.BlockSpec((w, t), lambda i: (0, i))
    pos = pl.BlockSpec((t, HEAD_DIM), lambda i: (i % tiles_per_seq, 0))
    tok_widths = (SG_WIDTH, SG_WIDTH, KV_W, D_MODEL, D_MODEL)
    ch_widths = (Q_W, KV_W)
    return pl.pallas_call(
        _inproj_kernel,
        grid=(n // t,),
        in_specs=[tok(D_MODEL), _const_spec((1, D_MODEL)),
                  _const_spec((D_MODEL, D_IN)), _const_spec((CH_W, D_MODEL)),
                  _const_spec((1, SG_WIDTH)), _const_spec((1, SG_WIDTH)),
                  pos, pos],
        out_specs=[tok(w) for w in tok_widths] + [chan(w) for w in ch_widths],
        out_shape=([jax.ShapeDtypeStruct((n, w), BF16) for w in tok_widths]
                   + [jax.ShapeDtypeStruct((w, n), BF16) for w in ch_widths]),
        compiler_params=pltpu.CompilerParams(
            dimension_semantics=("parallel",), vmem_limit_bytes=VMEM_LIMIT_BYTES),
        name="inproj",
    )(x2d, g_mix, w_tok, w_ch, ln_g, ln_b, *rope)


def _mixer_kernel(sink_ref, x_ref, u_ref, v_ref, qt_ref,
                  kc_ref, kp_ref, kn_ref, vtc_ref, vtp_ref, vtn_ref, ga_ref, gb_ref,
                  ws_ref, bs_ref, wa_ref, wb_ref, wout_ref, o_ref,
                  attn_scr, sg_scr):
    t = MIXER_TILE
    nblk = t // BLOCK
    i = pl.program_id(1)
    first_tile = i == 0
    last_tile = i == pl.num_programs(1) - 1

    assert BLOCK == CHUNK

    def blk(j):
        return slice(j * BLOCK, (j + 1) * BLOCK)

    def spatial_gate(j):
        rows = blk(j)
        for g in range(SG_GROUPS):
            cols = slice(g * SG_GROUP_DIM, (g + 1) * SG_GROUP_DIM)
            mixed = jnp.dot(ws_ref[g], v_ref[rows, cols], preferred_element_type=F32)
            mixed = mixed + bs_ref[:, g:g + 1]
            sg_scr[rows, cols] = (u_ref[rows, cols].astype(F32) * mixed).astype(BF16)

    def keys(j, hk):
        cols = slice(hk * HEAD_DIM, (hk + 1) * HEAD_DIM)
        if 0 < j < nblk - 1:
            return kc_ref[(j - 1) * BLOCK:(j + 2) * BLOCK, cols]
        prev = kp_ref[:, cols] if j == 0 else kc_ref[blk(j - 1), cols]
        nxt = kn_ref[:, cols] if j == nblk - 1 else kc_ref[blk(j + 1), cols]
        return jnp.concatenate([prev, kc_ref[blk(j), cols], nxt], axis=0)

    def values_t(j, hk):
        rows = slice(hk * HEAD_DIM, (hk + 1) * HEAD_DIM)
        if 0 < j < nblk - 1:
            return vtc_ref[rows, (j - 1) * BLOCK:(j + 2) * BLOCK]
        prev = vtp_ref[rows, :] if j == 0 else vtc_ref[rows, blk(j - 1)]
        nxt = vtn_ref[rows, :] if j == nblk - 1 else vtc_ref[rows, blk(j + 1)]
        return jnp.concatenate([prev, vtc_ref[rows, blk(j)], nxt], axis=1)

    lanes_q = Q_PER_KV * BLOCK
    kr = lax.broadcasted_iota(jnp.int32, (BLOCK, lanes_q), 0)
    qc = lax.broadcasted_iota(jnp.int32, (BLOCK, lanes_q), 1) % BLOCK
    prev_ok = kr >= qc
    next_ok = kr <= qc
    prev_ok_first = kr >= qc + jnp.where(first_tile, BLOCK, 0)
    next_ok_last = kr <= qc - jnp.where(last_tile, BLOCK, 0)

    def heads_of(hk):
        return [hk * Q_PER_KV + g for g in range(Q_PER_KV)]

    def scores(j, hk):
        kk = keys(j, hk)
        qt4 = jnp.concatenate(
            [qt_ref[hd * HEAD_DIM:(hd + 1) * HEAD_DIM, blk(j)] for hd in heads_of(hk)], axis=1)
        return jnp.dot(kk, qt4, preferred_element_type=F32)

    def attend(j, hk, s):
        p_ok = prev_ok_first if j == 0 else prev_ok
        n_ok = next_ok_last if j == nblk - 1 else next_ok
        heads = heads_of(hk)
        s_p = jnp.where(p_ok, s[:BLOCK], NEG)
        s_c = s[BLOCK:2 * BLOCK]
        s_n = jnp.where(n_ok, s[2 * BLOCK:], NEG)
        sink = jnp.concatenate(
            [jnp.full((1, BLOCK), sink_ref[hd] * LOG2_E, F32) for hd in heads], axis=1)
        m = jnp.maximum(jnp.maximum(s_p, s_c), s_n).max(axis=0, keepdims=True)
        m = jnp.maximum(m, sink)
        p = jnp.concatenate(
            [jnp.exp2(s_p - m), jnp.exp2(s_c - m), jnp.exp2(s_n - m)], axis=0).astype(BF16)
        vt = values_t(j, hk)
        vt1 = jnp.concatenate([vt, jnp.ones((ONES_ROWS, 3 * BLOCK), BF16)], axis=0)
        ot1 = jnp.dot(vt1, p, preferred_element_type=F32)
        denom = ot1[HEAD_DIM:HEAD_DIM + 1] + jnp.exp2(sink - m)
        ot = ot1[:HEAD_DIM] / denom
        for g, hd in enumerate(heads):
            attn_scr[blk(j), hd * HEAD_DIM:(hd + 1) * HEAD_DIM] = (
                ot[:, g * BLOCK:(g + 1) * BLOCK].T.astype(BF16))

    gb_ = MIXER_PROJ_BLOCKS
    ngrp = nblk // gb_

    def grp(g):
        return slice(g * gb_ * BLOCK, (g + 1) * gb_ * BLOCK)

    def proj_a(g):
        return jnp.dot(sg_scr[grp(g), :], wa_ref[...], preferred_element_type=F32)

    def proj_b(g):
        return jnp.dot(attn_scr[grp(g), :], wb_ref[...], preferred_element_type=F32)

    def proj_out(g, a, b):
        merged = (ga_ref[grp(g), :].astype(F32) * a + gb_ref[grp(g), :].astype(F32) * b)
        o_ref[grp(g), :] = x_ref[grp(g), :] + jnp.dot(
            merged.astype(BF16), wout_ref[...], preferred_element_type=F32)

    a_prev = b_prev = None
    for j in range(nblk):
        g, first_of_group, last_of_group = j // gb_, j % gb_ == 0, j % gb_ == gb_ - 1
        s0 = scores(j, 0)
        s1 = scores(j, 1)
        spatial_gate(j)
        if first_of_group and g > 0:
            b_prev = proj_b(g - 1)
        if last_of_group:
            a_cur = proj_a(g)
        attend(j, 0, s0)
        if first_of_group and g > 0:
            proj_out(g - 1, a_prev, b_prev)
        attend(j, 1, s1)
        if last_of_group:
            a_prev = a_cur
    proj_out(ngrp - 1, a_prev, proj_b(ngrp - 1))


def _mixer(x2d, seq, u, v, qt, k, vat, ga, gb, sg_w, sg_bt, sink, w_a, w_b, w_out):
    n = x2d.shape[0]
    t = MIXER_TILE
    tiles_per_seq = seq // t
    blocks_per_tile = t // BLOCK
    blocks_per_seq = seq // BLOCK
    bsz = n // seq

    def tile_idx(b, i):
        return b * tiles_per_seq + i

    def prev_idx(b, i):
        return b * blocks_per_seq + jnp.maximum(i * blocks_per_tile - 1, 0)

    def next_idx(b, i):
        return b * blocks_per_seq + jnp.minimum((i + 1) * blocks_per_tile, blocks_per_seq - 1)

    def tok(w):
        return pl.BlockSpec((t, w), lambda b, i: (tile_idx(b, i), 0))

    return pl.pallas_call(
        _mixer_kernel,
        grid=(bsz, tiles_per_seq),
        in_specs=[pl.BlockSpec(memory_space=pltpu.SMEM),
                  tok(D_MODEL), tok(SG_WIDTH), tok(SG_WIDTH),
                  pl.BlockSpec((Q_W, t), lambda b, i: (0, tile_idx(b, i))),
                  tok(KV_W),
                  pl.BlockSpec((BLOCK, KV_W), lambda b, i: (prev_idx(b, i), 0)),
                  pl.BlockSpec((BLOCK, KV_W), lambda b, i: (next_idx(b, i), 0)),
                  pl.BlockSpec((KV_W, t), lambda b, i: (0, tile_idx(b, i))),
                  pl.BlockSpec((KV_W, BLOCK), lambda b, i: (0, prev_idx(b, i))),
                  pl.BlockSpec((KV_W, BLOCK), lambda b, i: (0, next_idx(b, i))),
                  tok(D_MODEL), tok(D_MODEL),
                  _const_spec((SG_GROUPS, CHUNK, CHUNK)), _const_spec((CHUNK, SG_GROUPS)),
                  _const_spec((SG_WIDTH, D_MODEL)), _const_spec((Q_W, D_MODEL)),
                  _const_spec((D_MODEL, D_MODEL))],
        out_specs=tok(D_MODEL),
        scratch_shapes=[pltpu.VMEM((t, Q_W), BF16),
                        pltpu.VMEM((t, SG_WIDTH), BF16)],
        out_shape=jax.ShapeDtypeStruct((n, D_MODEL), F32),
        compiler_params=pltpu.CompilerParams(
            dimension_semantics=("parallel", "parallel"), vmem_limit_bytes=VMEM_LIMIT_BYTES),
        name="mixer",
    )(sink, x2d, u, v, qt, k, k, k, vat, vat, vat, ga, gb, sg_w, sg_bt, w_a, w_b, w_out)


def _ffn_kernel(x_ref, gffn_ref, wg_ref, wu_ref, wd_ref, gfin_ref, o_ref):
    n_pass = len(FF_BOUNDS) - 1
    stages = [(r, c) for r in range(FFN_TILE // FFN_SUB) for c in range(n_pass)]
    xs, hs, accs = {}, {}, {}

    def gate_up(r, c):
        rows = slice(r * FFN_SUB, (r + 1) * FFN_SUB)
        if c == 0:
            xs[r] = x_ref[rows, :]
            hs[r] = (xs[r] * _rms_scale(xs[r]) * gffn_ref[...]).astype(BF16)
            accs[r] = xs[r]
        cols = slice(FF_BOUNDS[c], FF_BOUNDS[c + 1])
        gate = jnp.dot(hs[r], wg_ref[:, cols], preferred_element_type=F32)
        up = jnp.dot(hs[r], wu_ref[:, cols], preferred_element_type=F32)
        return (jax.nn.silu(gate) * up).astype(BF16)

    def down(r, c, act):
        rows = slice(r * FFN_SUB, (r + 1) * FFN_SUB)
        cols = slice(FF_BOUNDS[c], FF_BOUNDS[c + 1])
        accs[r] = accs[r] + jnp.dot(act, wd_ref[cols, :], preferred_element_type=F32)
        if c == n_pass - 1:
            o_ref[rows, :] = accs[r] * _rms_scale(accs[r]) * gfin_ref[...]

    act = gate_up(*stages[0])
    for s, stage in enumerate(stages):
        act_next = gate_up(*stages[s + 1]) if s + 1 < len(stages) else None
        down(*stage, act)
        act = act_next


def _ffn(x2d, g_ffn, w_gate, w_up, w_down, g_final):
    n = x2d.shape[0]
    t = FFN_TILE
    tok = pl.BlockSpec((t, D_MODEL), lambda i: (i, 0))
    return pl.pallas_call(
        _ffn_kernel,
        grid=(n // t,),
        in_specs=[tok, _const_spec((1, D_MODEL)), _const_spec((D_MODEL, D_FF)),
                  _const_spec((D_MODEL, D_FF)), _const_spec((D_FF, D_MODEL)),
                  _const_spec((1, D_MODEL))],
        out_specs=tok,
        out_shape=jax.ShapeDtypeStruct((n, D_MODEL), F32),
        compiler_params=pltpu.CompilerParams(
            dimension_semantics=("parallel",), vmem_limit_bytes=VMEM_LIMIT_BYTES),
        name="ffn",
    )(x2d, g_ffn, w_gate, w_up, w_down, g_final)


def _rope_tables(seq):
    half = HEAD_DIM // 2
    assert seq % ROPE_LO == 0
    inv = ROPE_THETA ** (-jnp.arange(half, dtype=F32) / half)
    inv2 = jnp.concatenate([inv, inv])
    a_hi = (jnp.arange(seq // ROPE_LO, dtype=F32) * ROPE_LO)[:, None] * inv2[None, :]
    a_lo = jnp.arange(ROPE_LO, dtype=F32)[:, None] * inv2[None, :]
    ch, sh = jnp.cos(a_hi)[:, None, :], jnp.sin(a_hi)[:, None, :]
    cl, sl = jnp.cos(a_lo)[None], jnp.sin(a_lo)[None]
    sign = jnp.concatenate([-jnp.ones((half,), F32), jnp.ones((half,), F32)])
    cos = (ch * cl - sh * sl).reshape(seq, HEAD_DIM)
    sin = ((sh * cl + ch * sl) * sign).reshape(seq, HEAD_DIM)
    return cos, sin


def _trunk(x, p, rope):
    bsz, seq, _ = x.shape
    assert seq % INPROJ_TILE == 0 and seq % MIXER_TILE == 0 and (bsz * seq) % FFN_TILE == 0
    assert MIXER_TILE % BLOCK == 0 and BLOCK == CHUNK
    x2d = x.reshape(bsz * seq, D_MODEL)
    u, v, k, ga, gb, qt, vat = _inproj(x2d, seq, p["g_mix"], p["w_tok"], p["w_ch"],
                                       p["ln_g"], p["ln_b"], rope)
    x2d = _mixer(x2d, seq, u, v, qt, k, vat, ga, gb, p["sg_w"], p["sg_bt"], p["sink"],
                 p["w_a"], p["w_b"], p["w_out"])
    x2d = _ffn(x2d, p["g_ffn"], p["w_gate"], p["w_up"], p["w_down"], p["g_final"])
    return x2d.reshape(bsz, seq, D_MODEL)


def kernel(x_prompt, x_sample, g_mix, w_in, sg_ln_g, sg_ln_b, sg_w, sg_b, attn_sink,
           w_a, w_b, w_out, g_ffn, w_gate, w_up, w_down, g_final):
    assert w_in.shape[0] == 1, "single layer: the final rmsnorm is fused into its channel mixer"
    row = lambda a: a.reshape(1, -1).astype(F32)
    w = w_in[0]
    w_ch = _channel_major_weights(w)
    p = dict(
        g_mix=row(g_mix[0]), w_tok=w.astype(BF16), w_ch=w_ch,
        ln_g=row(sg_ln_g[0]), ln_b=row(sg_ln_b[0]),
        sg_w=sg_w[0].astype(BF16), sg_bt=sg_b[0].T.astype(F32),
        sink=attn_sink[0].astype(F32),
        w_a=w_a[0].astype(BF16), w_b=w_b[0].astype(BF16), w_out=w_out[0].astype(BF16),
        g_ffn=row(g_ffn[0]), w_gate=w_gate[0].astype(BF16), w_up=w_up[0].astype(BF16),
        w_down=w_down[0].astype(BF16), g_final=row(g_final))
    rope = _rope_tables(max(x_prompt.shape[1], x_sample.shape[1]))
    return (_trunk(x_prompt, p, rope), _trunk(x_sample, p, rope))
```

```python
import jax
import jax.numpy as jnp
from jax import lax
from jax.experimental import pallas as pl
from jax.experimental.pallas import tpu as pltpu

D_MODEL = 1024
HEAD_DIM = 128
N_Q_HEADS = D_MODEL // HEAD_DIM
N_KV_HEADS = 2
Q_PER_KV = N_Q_HEADS // N_KV_HEADS
WINDOW = 128
BLOCK = 128
ROPE_THETA = 10000.0
SG_WIDTH = D_MODEL
SG_GROUPS = 4
SG_GROUP_DIM = SG_WIDTH // SG_GROUPS
CHUNK = 128
D_FF = -(-8 * D_MODEL // (3 * 256)) * 256
Q_W = N_Q_HEADS * HEAD_DIM
KV_W = N_KV_HEADS * HEAD_DIM
D_IN = 2 * SG_WIDTH + Q_W + 2 * KV_W + 2 * D_MODEL
RMS_EPS = 1e-6
LN_EPS = 1e-5
NEG = -1e30
LOG2_E = 1.4426950408889634

OFF_U = 0
OFF_V = OFF_U + SG_WIDTH
OFF_Q = OFF_V + SG_WIDTH
OFF_K = OFF_Q + Q_W
OFF_VA = OFF_K + KV_W
OFF_GA = OFF_VA + KV_W
OFF_GB = OFF_GA + D_MODEL

CH_W = Q_W + KV_W

V7X_VMEM_BYTES = 64 * 1024 * 1024
VMEM_LIMIT_BYTES = V7X_VMEM_BYTES - 8 * 1024 * 1024

INPROJ_TILE = 1024
INPROJ_SUB = 256
MIXER_TILE = 1024
ONES_ROWS = 16
MIXER_PROJ_BLOCKS = 2
FFN_TILE = 1024
FFN_SUB = 256
ROPE_LO = 64
V7X_MXU_DIM = 256
FF_TILES = D_FF // V7X_MXU_DIM
FF_BOUNDS = (0, (FF_TILES // 2) * V7X_MXU_DIM, D_FF)

BF16 = jnp.bfloat16
F32 = jnp.float32

NT_DIMS = (((1,), (1,)), ((), ()))


def _const_spec(shape):
    nd = len(shape)
    return pl.BlockSpec(shape, lambda *_: (0,) * nd, pipeline_mode=pl.Buffered(1))


def _rms_scale(x):
    return lax.rsqrt(jnp.mean(x * x, axis=-1, keepdims=True) + RMS_EPS)


def _wt_kernel(w_ref, o_ref):
    o_ref[...] = w_ref[...].T.astype(BF16)


def _channel_major_weights(w):
    cb = KV_W
    assert OFF_Q % cb == 0 and OFF_VA % cb == 0 and Q_W % cb == 0
    n_q = Q_W // cb
    return pl.pallas_call(
        _wt_kernel,
        grid=(CH_W // cb,),
        in_specs=[pl.BlockSpec(
            (D_MODEL, cb), lambda i: (0, jnp.where(i < n_q, OFF_Q // cb + i, OFF_VA // cb)))],
        out_specs=pl.BlockSpec((cb, D_MODEL), lambda i: (i, 0)),
        out_shape=jax.ShapeDtypeStruct((CH_W, D_MODEL), BF16),
        compiler_params=pltpu.CompilerParams(dimension_semantics=("parallel",)),
        name="w_in_channel_major",
    )(w)


def _inproj_kernel(x_ref, gmix_ref, wtok_ref, wch_ref, lng_ref, lnb_ref,
                   cos_ref, sin_ref,
                   u_ref, v_ref, k_ref, ga_ref, gb_ref, qt_ref, vat_ref):
    half = HEAD_DIM // 2
    for r in range(INPROJ_TILE // INPROJ_SUB):
        rows = slice(r * INPROJ_SUB, (r + 1) * INPROJ_SUB)
        x = x_ref[rows, :]
        h = (x * _rms_scale(x) * gmix_ref[...]).astype(BF16)

        def proj(off, width):
            return jnp.dot(h, wtok_ref[:, off:off + width], preferred_element_type=F32)

        v = proj(OFF_V, SG_WIDTH)
        mu = jnp.mean(v, axis=-1, keepdims=True)
        vc = v - mu
        vn = vc * lax.rsqrt(jnp.mean(vc * vc, axis=-1, keepdims=True) + LN_EPS)
        v_ref[rows, :] = (vn * lng_ref[...] + lnb_ref[...]).astype(BF16)

        ga_ref[rows, :] = jax.nn.sigmoid(proj(OFF_GA, D_MODEL)).astype(BF16)

        zt = lax.dot_general(wch_ref[...], h, NT_DIMS, preferred_element_type=F32)
        scale = HEAD_DIM ** -0.5 * LOG2_E
        cq = cos_ref[rows, :].T[:half] * scale
        sq = sin_ref[rows, :].T[half:] * scale
        for hd in range(N_Q_HEADS):
            x1 = zt[hd * HEAD_DIM:hd * HEAD_DIM + half, :]
            x2 = zt[hd * HEAD_DIM + half:(hd + 1) * HEAD_DIM, :]
            qt_ref[hd * HEAD_DIM:hd * HEAD_DIM + half, rows] = (x1 * cq - x2 * sq).astype(BF16)
            qt_ref[hd * HEAD_DIM + half:(hd + 1) * HEAD_DIM, rows] = (
                x2 * cq + x1 * sq).astype(BF16)
        vat_ref[:, rows] = zt[Q_W:, :].astype(BF16)

        gb_ref[rows, :] = jax.nn.sigmoid(proj(OFF_GB, D_MODEL)).astype(BF16)

        u_ref[rows, :] = proj(OFF_U, SG_WIDTH).astype(BF16)

        cos = cos_ref[rows, :]
        sin = sin_ref[rows, :]
        k = proj(OFF_K, KV_W)
        for hd in range(N_KV_HEADS):
            sl = slice(hd * HEAD_DIM, (hd + 1) * HEAD_DIM)
            kh = k[:, sl]
            k_ref[rows, sl] = (kh * cos + pltpu.roll(kh, half, axis=1) * sin).astype(BF16)


def _inproj(x2d, seq, g_mix, w_tok, w_ch, ln_g, ln_b, rope):
    n = x2d.shape[0]
    t = INPROJ_TILE
    tiles_per_seq = seq // t
    tok = lambda w: pl.BlockSpec((t, w), lambda i: (i, 0))
    chan = lambda w: pl.BlockSpec((w, t), lambda i: (0, i))
    pos = pl.BlockSpec((t, HEAD_DIM), lambda i: (i % tiles_per_seq, 0))
    tok_widths = (SG_WIDTH, SG_WIDTH, KV_W, D_MODEL, D_MODEL)
    ch_widths = (Q_W, KV_W)
    return pl.pallas_call(
        _inproj_kernel,
        grid=(n // t,),
        in_specs=[tok(D_MODEL), _const_spec((1, D_MODEL)),
                  _const_spec((D_MODEL, D_IN)), _const_spec((CH_W, D_MODEL)),
                  _const_spec((1, SG_WIDTH)), _const_spec((1, SG_WIDTH)),
                  pos, pos],
        out_specs=[tok(w) for w in tok_widths] + [chan(w) for w in ch_widths],
        out_shape=([jax.ShapeDtypeStruct((n, w), BF16) for w in tok_widths]
                   + [jax.ShapeDtypeStruct((w, n), BF16) for w in ch_widths]),
        compiler_params=pltpu.CompilerParams(
            dimension_semantics=("parallel",), vmem_limit_bytes=VMEM_LIMIT_BYTES),
        name="inproj",
    )(x2d, g_mix, w_tok, w_ch, ln_g, ln_b, *rope)


def _mixer_kernel(sink_ref, x_ref, u_ref, v_ref, qt_ref,
                  kc_ref, kp_ref, kn_ref, vtc_ref, vtp_ref, vtn_ref, ga_ref, gb_ref,
                  ws_ref, bs_ref, wa_ref, wb_ref, wout_ref, o_ref,
                  attn_scr, sg_scr):
    t = MIXER_TILE
    nblk = t // BLOCK
    i = pl.program_id(1)
    first_tile = i == 0
    last_tile = i == pl.num_programs(1) - 1

    assert BLOCK == CHUNK

    def blk(j):
        return slice(j * BLOCK, (j + 1) * BLOCK)

    def spatial_gate(j):
        rows = blk(j)
        for g in range(SG_GROUPS):
            cols = slice(g * SG_GROUP_DIM, (g + 1) * SG_GROUP_DIM)
            mixed = jnp.dot(ws_ref[g], v_ref[rows, cols], preferred_element_type=F32)
            mixed = mixed + bs_ref[:, g:g + 1]
            sg_scr[rows, cols] = (u_ref[rows, cols].astype(F32) * mixed).astype(BF16)

    def keys(j, hk):
        cols = slice(hk * HEAD_DIM, (hk + 1) * HEAD_DIM)
        if 0 < j < nblk - 1:
            return kc_ref[(j - 1) * BLOCK:(j + 2) * BLOCK, cols]
        prev = kp_ref[:, cols] if j == 0 else kc_ref[blk(j - 1), cols]
        nxt = kn_ref[:, cols] if j == nblk - 1 else kc_ref[blk(j + 1), cols]
        return jnp.concatenate([prev, kc_ref[blk(j), cols], nxt], axis=0)

    def values_t(j, hk):
        rows = slice(hk * HEAD_DIM, (hk + 1) * HEAD_DIM)
        if 0 < j < nblk - 1:
            return vtc_ref[rows, (j - 1) * BLOCK:(j + 2) * BLOCK]
        prev = vtp_ref[rows, :] if j == 0 else vtc_ref[rows, blk(j - 1)]
        nxt = vtn_ref[rows, :] if j == nblk - 1 else vtc_ref[rows, blk(j + 1)]
        return jnp.concatenate([prev, vtc_ref[rows, blk(j)], nxt], axis=1)

    lanes_q = Q_PER_KV * BLOCK
    kr = lax.broadcasted_iota(jnp.int32, (BLOCK, lanes_q), 0)
    qc = lax.broadcasted_iota(jnp.int32, (BLOCK, lanes_q), 1) % BLOCK
    prev_ok = kr >= qc
    next_ok = kr <= qc
    prev_ok_first = kr >= qc + jnp.where(first_tile, BLOCK, 0)
    next_ok_last = kr <= qc - jnp.where(last_tile, BLOCK, 0)

    def heads_of(hk):
        return [hk * Q_PER_KV + g for g in range(Q_PER_KV)]

    def scores(j, hk):
        kk = keys(j, hk)
        qt4 = jnp.concatenate(
            [qt_ref[hd * HEAD_DIM:(hd + 1) * HEAD_DIM, blk(j)] for hd in heads_of(hk)], axis=1)
        return jnp.dot(kk, qt4, preferred_element_type=F32)

    def attend(j, hk, s):
        p_ok = prev_ok_first if j == 0 else prev_ok
        n_ok = next_ok_last if j == nblk - 1 else next_ok
        heads = heads_of(hk)
        s_p = jnp.where(p_ok, s[:BLOCK], NEG)
        s_c = s[BLOCK:2 * BLOCK]
        s_n = jnp.where(n_ok, s[2 * BLOCK:], NEG)
        sink = jnp.concatenate(
            [jnp.full((1, BLOCK), sink_ref[hd] * LOG2_E, F32) for hd in heads], axis=1)
        m = jnp.maximum(jnp.maximum(s_p, s_c), s_n).max(axis=0, keepdims=True)
        m = jnp.maximum(m, sink)
        p = jnp.concatenate(
            [jnp.exp2(s_p - m), jnp.exp2(s_c - m), jnp.exp2(s_n - m)], axis=0).astype(BF16)
        vt = values_t(j, hk)
        vt1 = jnp.concatenate([vt, jnp.ones((ONES_ROWS, 3 * BLOCK), BF16)], axis=0)
        ot1 = jnp.dot(vt1, p, preferred_element_type=F32)
        denom = ot1[HEAD_DIM:HEAD_DIM + 1] + jnp.exp2(sink - m)
        ot = ot1[:HEAD_DIM] / denom
        for g, hd in enumerate(heads):
            attn_scr[blk(j), hd * HEAD_DIM:(hd + 1) * HEAD_DIM] = (
                ot[:, g * BLOCK:(g + 1) * BLOCK].T.astype(BF16))

    gb_ = MIXER_PROJ_BLOCKS
    ngrp = nblk // gb_

    def grp(g, part=None):
        if part is None:
            return slice(g * gb_ * BLOCK, (g + 1) * gb_ * BLOCK)
        return blk(g * gb_ + part)

    def proj_a(rows):
        return jnp.dot(sg_scr[rows, :], wa_ref[...], preferred_element_type=F32)

    def proj_b(rows):
        return jnp.dot(attn_scr[rows, :], wb_ref[...], preferred_element_type=F32)

    def proj_out(rows, a, b):
        merged = (ga_ref[rows, :].astype(F32) * a + gb_ref[rows, :].astype(F32) * b)
        o_ref[rows, :] = x_ref[rows, :] + jnp.dot(
            merged.astype(BF16), wout_ref[...], preferred_element_type=F32)

    assert nblk == 8 and gb_ == 2, "the list schedule below is written for 8 blocks, groups of 2"
    last = ngrp - 1
    schedule = (
        ("S", 0), ("SG", 0), ("SG", 1), ("SG", 2), ("SG", 3), ("S", 1), ("A", 0), ("PV", 0),
        ("SG", 4), ("SG", 5), ("S", 2), ("PV", 1), ("A", 1), ("B", 0), ("S", 3), ("PV", 2),
        ("OUT", 0), ("SG", 6), ("SG", 7), ("S", 4), ("PV", 3), ("A", 2), ("B", 1), ("S", 5),
        ("PV", 4), ("OUT", 1), ("S", 6), ("PV", 5), ("S", 7), ("A", 3), ("B", 2), ("PV", 6),
        ("PV", 7), ("OUT", 2), ("B", last, 0), ("B", last, 1), ("OUT", last, 0),
        ("OUT", last, 1))
    s_val, a_val, b_val = {}, {}, {}
    for op, idx, *part in schedule:
        if op == "S":
            s_val[idx] = (scores(idx, 0), scores(idx, 1))
        elif op == "SG":
            spatial_gate(idx)
        elif op == "PV":
            attend(idx, 0, s_val[idx][0])
            attend(idx, 1, s_val[idx][1])
        elif op == "A":
            a_val[idx] = proj_a(grp(idx))
        elif op == "B":
            b_val[(idx, *part)] = proj_b(grp(idx, *part))
        elif part:
            a_rows = a_val[idx][part[0] * BLOCK:(part[0] + 1) * BLOCK]
            proj_out(grp(idx, *part), a_rows, b_val[(idx, *part)])
        else:
            proj_out(grp(idx), a_val[idx], b_val[(idx,)])


def _mixer(x2d, seq, u, v, qt, k, vat, ga, gb, sg_w, sg_bt, sink, w_a, w_b, w_out):
    n = x2d.shape[0]
    t = MIXER_TILE
    tiles_per_seq = seq // t
    blocks_per_tile = t // BLOCK
    blocks_per_seq = seq // BLOCK
    bsz = n // seq

    def tile_idx(b, i):
        return b * tiles_per_seq + i

    def prev_idx(b, i):
        return b * blocks_per_seq + jnp.maximum(i * blocks_per_tile - 1, 0)

    def next_idx(b, i):
        return b * blocks_per_seq + jnp.minimum((i + 1) * blocks_per_tile, blocks_per_seq - 1)

    def tok(w):
        return pl.BlockSpec((t, w), lambda b, i: (tile_idx(b, i), 0))

    return pl.pallas_call(
        _mixer_kernel,
        grid=(bsz, tiles_per_seq),
        in_specs=[pl.BlockSpec(memory_space=pltpu.SMEM),
                  tok(D_MODEL), tok(SG_WIDTH), tok(SG_WIDTH),
                  pl.BlockSpec((Q_W, t), lambda b, i: (0, tile_idx(b, i))),
                  tok(KV_W),
                  pl.BlockSpec((BLOCK, KV_W), lambda b, i: (prev_idx(b, i), 0)),
                  pl.BlockSpec((BLOCK, KV_W), lambda b, i: (next_idx(b, i), 0)),
                  pl.BlockSpec((KV_W, t), lambda b, i: (0, tile_idx(b, i))),
                  pl.BlockSpec((KV_W, BLOCK), lambda b, i: (0, prev_idx(b, i))),
                  pl.BlockSpec((KV_W, BLOCK), lambda b, i: (0, next_idx(b, i))),
                  tok(D_MODEL), tok(D_MODEL),
                  _const_spec((SG_GROUPS, CHUNK, CHUNK)), _const_spec((CHUNK, SG_GROUPS)),
                  _const_spec((SG_WIDTH, D_MODEL)), _const_spec((Q_W, D_MODEL)),
                  _const_spec((D_MODEL, D_MODEL))],
        out_specs=tok(D_MODEL),
        scratch_shapes=[pltpu.VMEM((t, Q_W), BF16),
                        pltpu.VMEM((t, SG_WIDTH), BF16)],
        out_shape=jax.ShapeDtypeStruct((n, D_MODEL), F32),
        compiler_params=pltpu.CompilerParams(
            dimension_semantics=("parallel", "parallel"), vmem_limit_bytes=VMEM_LIMIT_BYTES),
        name="mixer",
    )(sink, x2d, u, v, qt, k, k, k, vat, vat, vat, ga, gb, sg_w, sg_bt, w_a, w_b, w_out)


def _ffn_kernel(x_ref, gffn_ref, wg_ref, wu_ref, wd_ref, gfin_ref, o_ref):
    n_pass = len(FF_BOUNDS) - 1
    stages = [(r, c) for r in range(FFN_TILE // FFN_SUB) for c in range(n_pass)]
    xs, hs, accs = {}, {}, {}

    def gate_up(r, c):
        rows = slice(r * FFN_SUB, (r + 1) * FFN_SUB)
        if c == 0:
            xs[r] = x_ref[rows, :]
            hs[r] = (xs[r] * _rms_scale(xs[r]) * gffn_ref[...]).astype(BF16)
            accs[r] = xs[r]
        cols = slice(FF_BOUNDS[c], FF_BOUNDS[c + 1])
        gate = jnp.dot(hs[r], wg_ref[:, cols], preferred_element_type=F32)
        up = jnp.dot(hs[r], wu_ref[:, cols], preferred_element_type=F32)
        return (jax.nn.silu(gate) * up).astype(BF16)

    def down(r, c, act):
        rows = slice(r * FFN_SUB, (r + 1) * FFN_SUB)
        cols = slice(FF_BOUNDS[c], FF_BOUNDS[c + 1])
        accs[r] = accs[r] + jnp.dot(act, wd_ref[cols, :], preferred_element_type=F32)
        if c == n_pass - 1:
            o_ref[rows, :] = accs[r] * _rms_scale(accs[r]) * gfin_ref[...]

    act = gate_up(*stages[0])
    for s, stage in enumerate(stages):
        act_next = gate_up(*stages[s + 1]) if s + 1 < len(stages) else None
        down(*stage, act)
        act = act_next


def _ffn(x2d, g_ffn, w_gate, w_up, w_down, g_final):
    n = x2d.shape[0]
    t = FFN_TILE
    tok = pl.BlockSpec((t, D_MODEL), lambda i: (i, 0))
    return pl.pallas_call(
        _ffn_kernel,
        grid=(n // t,),
        in_specs=[tok, _const_spec((1, D_MODEL)), _const_spec((D_MODEL, D_FF)),
                  _const_spec((D_MODEL, D_FF)), _const_spec((D_FF, D_MODEL)),
                  _const_spec((1, D_MODEL))],
        out_specs=tok,
        out_shape=jax.ShapeDtypeStruct((n, D_MODEL), F32),
        compiler_params=pltpu.CompilerParams(
            dimension_semantics=("parallel",), vmem_limit_bytes=VMEM_LIMIT_BYTES),
        name="ffn",
    )(x2d, g_ffn, w_gate, w_up, w_down, g_final)


def _rope_tables(seq):
    half = HEAD_DIM // 2
    assert seq % ROPE_LO == 0
    inv = ROPE_THETA ** (-jnp.arange(half, dtype=F32) / half)
    inv2 = jnp.concatenate([inv, inv])
    a_hi = (jnp.arange(seq // ROPE_LO, dtype=F32) * ROPE_LO)[:, None] * inv2[None, :]
    a_lo = jnp.arange(ROPE_LO, dtype=F32)[:, None] * inv2[None, :]
    ch, sh = jnp.cos(a_hi)[:, None, :], jnp.sin(a_hi)[:, None, :]
    cl, sl = jnp.cos(a_lo)[None], jnp.sin(a_lo)[None]
    sign = jnp.concatenate([-jnp.ones((half,), F32), jnp.ones((half,), F32)])
    cos = (ch * cl - sh * sl).reshape(seq, HEAD_DIM)
    sin = ((sh * cl + ch * sl) * sign).reshape(seq, HEAD_DIM)
    return cos, sin


def _trunk(x, p, rope):
    bsz, seq, _ = x.shape
    assert seq % INPROJ_TILE == 0 and seq % MIXER_TILE == 0 and (bsz * seq) % FFN_TILE == 0
    assert MIXER_TILE % BLOCK == 0 and BLOCK == CHUNK
    x2d = x.reshape(bsz * seq, D_MODEL)
    u, v, k, ga, gb, qt, vat = _inproj(x2d, seq, p["g_mix"], p["w_tok"], p["w_ch"],
                                       p["ln_g"], p["ln_b"], rope)
    x2d = _mixer(x2d, seq, u, v, qt, k, vat, ga, gb, p["sg_w"], p["sg_bt"], p["sink"],
                 p["w_a"], p["w_b"], p["w_out"])
    x2d = _ffn(x2d, p["g_ffn"], p["w_gate"], p["w_up"], p["w_down"], p["g_final"])
    return x2d.reshape(bsz, seq, D_MODEL)


def kernel(x_prompt, x_sample, g_mix, w_in, sg_ln_g, sg_ln_b, sg_w, sg_b, attn_sink,
           w_a, w_b, w_out, g_ffn, w_gate, w_up, w_down, g_final):
    assert w_in.shape[0] == 1, "single layer: the final rmsnorm is fused into its channel mixer"
    row = lambda a: a.reshape(1, -1).astype(F32)
    w = w_in[0]
    w_ch = _channel_major_weights(w)
    p = dict(
        g_mix=row(g_mix[0]), w_tok=w.astype(BF16), w_ch=w_ch,
        ln_g=row(sg_ln_g[0]), ln_b=row(sg_ln_b[0]),
        sg_w=sg_w[0].astype(BF16), sg_bt=sg_b[0].T.astype(F32),
        sink=attn_sink[0].astype(F32),
        w_a=w_a[0].astype(BF16), w_b=w_b[0].astype(BF16), w_out=w_out[0].astype(BF16),
        g_ffn=row(g_ffn[0]), w_gate=w_gate[0].astype(BF16), w_up=w_up[0].astype(BF16),
        w_down=w_down[0].astype(BF16), g_final=row(g_final))
    rope = _rope_tables(max(x_prompt.shape[1], x_sample.shape[1]))
    return (_trunk(x_prompt, p, rope), _trunk(x_sample, p, rope))
```
